```python
import jax, jax.numpy as jnp
from jax import lax
import numpy as np

D_MODEL = 2048
BATCH = 8
SEQ = 4096
DEPTH = 2

N_META = 16
MIX_WIDTH = D_MODEL
CONV_WIDTH = MIX_WIDTH // 2
ATTN_WIDTH = MIX_WIDTH - CONV_WIDTH
HEAD_DIM = 128
N_ATTN_HEADS = ATTN_WIDTH // HEAD_DIM
CONV_GROUP = 128
N_CONV_GROUPS = CONV_WIDTH // CONV_GROUP
CONV_K = 3
D_FF = 4 * D_MODEL
Q_BLOCK = 128
EPS = 1e-6
IN_PROJ_WIDTH = 3 * CONV_WIDTH + 3 * ATTN_WIDTH

kernel_name = "hybrid_shortconv_stickbreaking_block"


def rms_norm(x, g):
    xf = x.astype(jnp.float32)
    y = xf * lax.rsqrt(jnp.mean(xf * xf, axis=-1, keepdims=True) + EPS)
    return (y * g.astype(jnp.float32)).astype(x.dtype)


def group_rms_norm(x, g, n_groups):
    lead = x.shape[:-1]
    c = x.shape[-1]
    xg = x.reshape(*lead, n_groups, c // n_groups)
    return rms_norm(xg, g.reshape(n_groups, c // n_groups)).reshape(*lead, c)


def causal_short_conv(u, w):
    length = u.shape[1]
    up = jnp.pad(u, ((0, 0), (CONV_K - 1, 0), (0, 0)))
    y = w[0] * up[:, 0:length]
    for tap in range(1, CONV_K):
        y = y + w[tap] * up[:, tap:tap + length]
    return y


def _stick_breaking_attend(q, k, v, q_pos, k_pos):
    z = jnp.einsum('bhqd,bhkd->bhqk', q, k).astype(jnp.float32) * (HEAD_DIM ** -0.5)
    visible = k_pos[None, :] < q_pos[:, None]
    log_1m_beta = jnp.where(visible, jax.nn.log_sigmoid(-z), 0.0)
    suffix = lax.cumsum(log_1m_beta, axis=3, reverse=True) - log_1m_beta
    a = jnp.where(visible, jnp.exp(jax.nn.log_sigmoid(z) + suffix), 0.0)
    return jnp.einsum('bhqk,bhkd->bhqd', a.astype(v.dtype), v)


def stick_breaking_attention(q, k, v):
    b, h, length, dh = q.shape
    meta_pos = jnp.arange(N_META)
    meta_out = _stick_breaking_attend(q[:, :, :N_META], k[:, :, :N_META], v[:, :, :N_META],
                                      meta_pos, meta_pos)
    n_blocks = (length - N_META) // Q_BLOCK
    q_blocks = q[:, :, N_META:].reshape(b, h, n_blocks, Q_BLOCK, dh).transpose(2, 0, 1, 3, 4)
    k_pos = jnp.arange(length)

    def one_block(args):
        q_blk, blk_idx = args
        q_pos = N_META + blk_idx * Q_BLOCK + jnp.arange(Q_BLOCK)
        return _stick_breaking_attend(q_blk, k, v, q_pos, k_pos)

    out = lax.map(one_block, (q_blocks, jnp.arange(n_blocks)))
    out = out.transpose(1, 2, 0, 3, 4).reshape(b, h, length - N_META, dh)
    return jnp.concatenate([meta_out, out], axis=2)


def hybrid_mixer(x, g_mix, w_in, conv_w, g_q, g_k, g_conv_out, g_attn_out, w_out):
    b, length, _ = x.shape
    hn = rms_norm(x, g_mix)
    proj = hn @ w_in
    splits = [CONV_WIDTH, 2 * CONV_WIDTH, 3 * CONV_WIDTH,
              3 * CONV_WIDTH + ATTN_WIDTH, 3 * CONV_WIDTH + 2 * ATTN_WIDTH]
    gate_b, gate_c, u, q, k, v = jnp.split(proj, splits, axis=-1)
    y_conv = gate_b * causal_short_conv(gate_c * u, conv_w)
    y_conv = group_rms_norm(y_conv, g_conv_out, N_CONV_GROUPS)
    q = rms_norm(q.reshape(b, length, N_ATTN_HEADS, HEAD_DIM), g_q).transpose(0, 2, 1, 3)
    k = rms_norm(k.reshape(b, length, N_ATTN_HEADS, HEAD_DIM), g_k).transpose(0, 2, 1, 3)
    v = v.reshape(b, length, N_ATTN_HEADS, HEAD_DIM).transpose(0, 2, 1, 3)
    y_attn = stick_breaking_attention(q, k, v).transpose(0, 2, 1, 3).reshape(b, length, ATTN_WIDTH)
    y_attn = group_rms_norm(y_attn, g_attn_out, N_ATTN_HEADS)
    return jnp.concatenate([y_conv, y_attn], axis=-1) @ w_out


def squared_relu_mlp(x, g_mlp, w_mlp_in, w_mlp_out):
    hn = rms_norm(x, g_mlp)
    return jnp.square(jax.nn.relu(hn @ w_mlp_in)) @ w_mlp_out


def setup_inputs(seed: int = 0) -> dict:
    key = jax.random.key(seed)
    ks = jax.random.split(key, 13)
    f32 = jnp.float32
    x = jax.random.normal(ks[0], (BATCH, SEQ, D_MODEL), f32)
    meta_tokens = jax.random.normal(ks[1], (N_META, D_MODEL), f32)
    g_mix = 1.0 + 0.02 * jax.random.normal(ks[2], (DEPTH, D_MODEL), f32)
    w_in = jax.random.normal(ks[3], (DEPTH, D_MODEL, IN_PROJ_WIDTH), f32) * D_MODEL ** -0.5
    conv_w = jax.random.normal(ks[4], (DEPTH, CONV_K, CONV_WIDTH), f32) * CONV_K ** -0.5
    g_q = 1.0 + 0.02 * jax.random.normal(ks[5], (DEPTH, HEAD_DIM), f32)
    g_k = 1.0 + 0.02 * jax.random.normal(ks[6], (DEPTH, HEAD_DIM), f32)
    g_conv_out = 1.0 + 0.02 * jax.random.normal(ks[7], (DEPTH, CONV_WIDTH), f32)
    g_attn_out = 1.0 + 0.02 * jax.random.normal(ks[8], (DEPTH, ATTN_WIDTH), f32)
    w_out = jax.random.normal(ks[9], (DEPTH, MIX_WIDTH, D_MODEL), f32) * MIX_WIDTH ** -0.5
    g_mlp = 1.0 + 0.02 * jax.random.normal(ks[10], (DEPTH, D_MODEL), f32)
    w_mlp_in = jax.random.normal(ks[11], (DEPTH, D_MODEL, D_FF), f32) * D_MODEL ** -0.5
    w_mlp_out = jax.random.normal(ks[12], (DEPTH, D_FF, D_MODEL), f32) * D_FF ** -0.5
    return {"x": x, "meta_tokens": meta_tokens, "g_mix": g_mix, "w_in": w_in,
            "conv_w": conv_w, "g_q": g_q, "g_k": g_k, "g_conv_out": g_conv_out,
            "g_attn_out": g_attn_out, "w_out": w_out, "g_mlp": g_mlp,
            "w_mlp_in": w_mlp_in, "w_mlp_out": w_mlp_out}


def reference(x, meta_tokens, g_mix, w_in, conv_w, g_q, g_k, g_conv_out, g_attn_out,
              w_out, g_mlp, w_mlp_in, w_mlp_out):
    b = x.shape[0]
    meta = jnp.broadcast_to(meta_tokens.astype(x.dtype)[None], (b, N_META, x.shape[-1]))
    h = jnp.concatenate([meta, x], axis=1)
    for i in range(DEPTH):
        h = h + hybrid_mixer(h, g_mix[i], w_in[i], conv_w[i], g_q[i], g_k[i],
                             g_conv_out[i], g_attn_out[i], w_out[i])
        h = h + squared_relu_mlp(h, g_mlp[i], w_mlp_in[i], w_mlp_out[i])
    return h[:, N_META:]
```

```python
import functools

import jax
import jax.numpy as jnp
from jax import lax
from jax.experimental import pallas as pl
from jax.experimental.pallas import tpu as pltpu

F32 = jnp.float32
BF16 = jnp.bfloat16

EPS = 1e-6
HEAD_DIM = 128
CONV_K = 3
SUBLANES = 8
META_ROWS = 128
COL_CHUNK = 2 * HEAD_DIM
VMEM_LIMIT = 56 * 1024 * 1024
NEG_BIG = -1e30


def _rms(x, g):
    ms = jnp.mean(x * x, axis=-1, keepdims=True)
    return x * lax.rsqrt(ms + EPS) * g


def _in_proj_kernel(x_ref, g_ref, w_ref, cw_ref, gco_ref, gqk_ref, ctx_ref,
                    yc_ref, q_ref, k_ref, v_ref, tail_ref,
                    hn_ref, cu_ref, carry_ref, *, tm, n_conv_steps, tiles_per_seq):
    i = pl.program_id(0)
    j = pl.program_id(1)
    cc = COL_CHUNK

    @pl.when(j == 0)
    def _():
        hn_ref[...] = _rms(x_ref[...], g_ref[...]).astype(BF16)

    acc = jnp.dot(hn_ref[...], w_ref[...], preferred_element_type=F32)

    @pl.when(j < n_conv_steps)
    def _conv():
        gate_b = acc[:, 0:cc]
        cu = acc[:, cc:2 * cc] * acc[:, 2 * cc:3 * cc]
        cu_ref[SUBLANES:SUBLANES + tm, :] = cu
        seq_start = (i % tiles_per_seq) == 0

        @pl.when(seq_start)
        def _():
            cu_ref[0:SUBLANES, :] = ctx_ref[0]

        @pl.when(jnp.logical_not(seq_start))
        def _():
            cu_ref[0:SUBLANES, :] = carry_ref[j]

        tail = cu[tm - SUBLANES:tm, :]
        carry_ref[j] = tail
        tail_ref[0, 0] = tail
        c1 = cu_ref[SUBLANES - 1:SUBLANES - 1 + tm, :]
        c2 = cu_ref[SUBLANES - 2:SUBLANES - 2 + tm, :]
        w = cw_ref[...]
        y = gate_b * (w[0:1, :] * c2 + w[1:2, :] * c1 + w[2:3, :] * cu)
        gco = gco_ref[...]
        for s in range(cc // HEAD_DIM):
            sl = slice(s * HEAD_DIM, (s + 1) * HEAD_DIM)
            yc_ref[:, sl] = _rms(y[:, sl], gco[:, sl]).astype(BF16)

    @pl.when(j >= n_conv_steps)
    def _qkv():
        gqk = gqk_ref[...]
        for s in range(cc // HEAD_DIM):
            sl = slice(s * HEAD_DIM, (s + 1) * HEAD_DIM)
            q_ref[:, sl] = _rms(acc[:, sl], gqk[0:1, :]).astype(BF16)
            k_ref[:, sl] = _rms(acc[:, cc + s * HEAD_DIM:cc + (s + 1) * HEAD_DIM],
                                gqk[1:2, :]).astype(BF16)
        v_ref[...] = acc[:, 2 * cc:3 * cc].astype(BF16)


def _in_proj(h, g_mix, w_perm, conv_w, g_co, g_qk, ctx_tail, *, tm, tiles_per_seq):
    m, d = h.shape
    cc = COL_CHUNK
    width = conv_w.shape[1]
    n_conv = width // cc
    n_steps = w_perm.shape[1] // (3 * cc)
    assert n_steps == 2 * n_conv and m % tm == 0

    def conv_col(i, j):
        return (0, jnp.minimum(j, n_conv - 1))

    def conv_out(i, j):
        return (i, jnp.minimum(j, n_conv - 1))

    def qkv_out(i, j):
        return (i, jnp.maximum(j - n_conv, 0))

    kern = functools.partial(_in_proj_kernel, tm=tm, n_conv_steps=n_conv,
                             tiles_per_seq=tiles_per_seq)
    act = jax.ShapeDtypeStruct((m, width), BF16)
    return pl.pallas_call(
        kern,
        grid=(m // tm, n_steps),
        in_specs=[
            pl.BlockSpec((tm, d), lambda i, j: (i, 0)),
            pl.BlockSpec((1, d), lambda i, j: (0, 0)),
            pl.BlockSpec((d, 3 * cc), lambda i, j: (0, j)),
            pl.BlockSpec((CONV_K, cc), conv_col),
            pl.BlockSpec((1, cc), conv_col),
            pl.BlockSpec((2, HEAD_DIM), lambda i, j: (0, 0)),
            pl.BlockSpec((1, SUBLANES, cc), lambda i, j: (jnp.minimum(j, n_conv - 1), 0, 0)),
        ],
        out_specs=[
            pl.BlockSpec((tm, cc), conv_out),
            pl.BlockSpec((tm, cc), qkv_out),
            pl.BlockSpec((tm, cc), qkv_out),
            pl.BlockSpec((tm, cc), qkv_out),
            pl.BlockSpec((1, 1, SUBLANES, cc),
                         lambda i, j: (i, jnp.minimum(j, n_conv - 1), 0, 0)),
        ],
        out_shape=[act, act, act, act,
                   jax.ShapeDtypeStruct((m // tm, n_conv, SUBLANES, cc), F32)],
        scratch_shapes=[
            pltpu.VMEM((tm, d), BF16),
            pltpu.VMEM((tm + SUBLANES, cc), F32),
            pltpu.VMEM((n_conv, SUBLANES, cc), F32),
        ],
        compiler_params=pltpu.CompilerParams(
            dimension_semantics=("arbitrary", "arbitrary"),
            vmem_limit_bytes=VMEM_LIMIT),
        name="in_proj",
    )(h, g_mix, w_perm, conv_w, g_co, g_qk, ctx_tail)


def _attn_kernel(*refs, seq, tq, own_pad, ctx_pad):
    has_ctx = ctx_pad is not None
    if has_ctx:
        q_ref, k_ref, v_ref, kc_ref, vc_ref, g_ref, o_ref, u_ref, acc_ref, r_ref = refs
    else:
        q_ref, k_ref, v_ref, g_ref, o_ref, u_ref, acc_ref, r_ref = refs
    tk = tq
    nq = seq // tq
    nt = (((1,), (1,)), ((), ()))

    rr = lax.broadcasted_iota(jnp.int32, (tk, tk), 0)
    cc = lax.broadcasted_iota(jnp.int32, (tk, tk), 1)
    u_ref[...] = jnp.where(rr >= cc, 1.0, 0.0).astype(BF16)
    diag_mask = cc < rr
    if own_pad:
        diag_mask = jnp.logical_and(diag_mask, cc >= own_pad)

    def block(q, kb, vb, u, mask):
        width = kb.shape[0]
        z = lax.dot_general(q, kb, nt, preferred_element_type=F32)
        sp = jnp.log(1.0 + jnp.exp(-jnp.abs(z)))
        l = jnp.minimum(-z, 0.0) - sp
        if mask is not None:
            l = jnp.where(mask, l, 0.0)
        hi = l.astype(BF16)
        lo = (l - hi.astype(F32)).astype(BF16)
        cs = (jnp.dot(hi, u, preferred_element_type=F32)
              + jnp.dot(lo, u, preferred_element_type=F32))
        r = r_ref[...]
        t = z + cs + jnp.concatenate([r] * (width // HEAD_DIM), axis=1)
        if mask is not None:
            t = jnp.where(mask, t, NEG_BIG)
        a = jnp.exp(t).astype(BF16)
        acc_ref[...] += jnp.dot(a, vb, preferred_element_type=F32)
        r_ref[...] = r + jnp.broadcast_to(cs[:, 0:1], (tq, HEAD_DIM))

    def q_block(qi):
        row0 = qi * tq if isinstance(qi, int) else pl.multiple_of(qi * tq, tq)
        q = q_ref[0, pl.ds(row0, tq), :]
        acc_ref[...] = jnp.zeros_like(acc_ref)
        r_ref[...] = jnp.zeros_like(r_ref)
        u = u_ref[...]
        block(q, k_ref[0, pl.ds(row0, tk), :], v_ref[0, pl.ds(row0, tk), :], u, diag_mask)

        if nq > 1:
            def kv_step(n, carry):
                c0 = pl.multiple_of((qi - 1 - n) * tk, tk)
                block(q, k_ref[0, pl.ds(c0, tk), :], v_ref[0, pl.ds(c0, tk), :], u, None)
                return carry
            lax.fori_loop(0, qi, kv_step, 0)

        if has_ctx:
            cw = kc_ref.shape[1]
            ctx_mask = lax.broadcasted_iota(jnp.int32, (tq, cw), 1) >= ctx_pad
            block(q, kc_ref[0], vc_ref[0], u_ref[0:cw, 0:cw], ctx_mask)

        o_ref[0, pl.ds(row0, tq), :] = _rms(acc_ref[...], g_ref[...]).astype(BF16)

    if nq == 1:
        q_block(0)
    else:
        def q_step(qi, carry):
            q_block(qi)
            return carry
        lax.fori_loop(0, nq, q_step, 0)


def _attention(q, k, v, ctx, g_ao, *, tq, own_pad, ctx_pad):
    b, seq, width = q.shape
    n_heads = width // HEAD_DIM
    blk = pl.BlockSpec((1, seq, HEAD_DIM), lambda bi, hi: (bi, 0, hi))
    in_specs = [blk, blk, blk]
    args = [q, k, v]
    if ctx is not None:
        kc, vc = ctx
        cblk = pl.BlockSpec((1, kc.shape[1], HEAD_DIM), lambda bi, hi: (0, 0, hi))
        in_specs += [cblk, cblk]
        args += [kc, vc]
    in_specs.append(pl.BlockSpec((1, HEAD_DIM), lambda bi, hi: (0, hi)))
    args.append(g_ao)
    kern = functools.partial(_attn_kernel, seq=seq, tq=tq, own_pad=own_pad,
                             ctx_pad=ctx_pad if ctx is not None else None)
    return pl.pallas_call(
        kern,
        grid=(b, n_heads),
        in_specs=in_specs,
        out_specs=blk,
        out_shape=jax.ShapeDtypeStruct((b, seq, width), BF16),
        scratch_shapes=[
            pltpu.VMEM((tq, tq), BF16),
            pltpu.VMEM((tq, HEAD_DIM), F32),
            pltpu.VMEM((tq, HEAD_DIM), F32),
        ],
        compiler_params=pltpu.CompilerParams(
            dimension_semantics=("arbitrary", "arbitrary"),
            vmem_limit_bytes=VMEM_LIMIT),
        name="sb_attention",
    )(*args)


def _out_proj_kernel(h_ref, yc_ref, ya_ref, w_ref, o_ref):
    half = yc_ref.shape[1]
    o_ref[...] = (h_ref[...]
                  + jnp.dot(yc_ref[...], w_ref[0:half, :], preferred_element_type=F32)
                  + jnp.dot(ya_ref[...], w_ref[half:, :], preferred_element_type=F32))


def _out_proj(h, yc, ya, w_out, *, tm):
    m, d = h.shape
    half = yc.shape[1]
    row = pl.BlockSpec((tm, d), lambda i: (i, 0))
    yrow = pl.BlockSpec((tm, half), lambda i: (i, 0))
    return pl.pallas_call(
        _out_proj_kernel,
        grid=(m // tm,),
        in_specs=[row, yrow, yrow, pl.BlockSpec(w_out.shape, lambda i: (0, 0))],
        out_specs=row,
        out_shape=jax.ShapeDtypeStruct((m, d), F32),
        compiler_params=pltpu.CompilerParams(
            dimension_semantics=("arbitrary",), vmem_limit_bytes=VMEM_LIMIT),
        name="out_proj",
    )(h, yc, ya, w_out)


def _mlp_kernel(h_ref, g_ref, w1_ref, w2_ref, o_ref, hn_ref):
    f = pl.program_id(1)

    @pl.when(f == 0)
    def _():
        x = h_ref[...]
        hn_ref[...] = _rms(x, g_ref[...]).astype(BF16)
        o_ref[...] = x

    hid = jnp.dot(hn_ref[...], w1_ref[...], preferred_element_type=F32)
    hid = jnp.square(jnp.maximum(hid, 0.0)).astype(BF16)
    o_ref[...] += jnp.dot(hid, w2_ref[...], preferred_element_type=F32)


def _mlp(h, g_mlp, w1, w2, *, tm, tf):
    m, d = h.shape
    d_ff = w1.shape[1]
    row = pl.BlockSpec((tm, d), lambda i, f: (i, 0))
    return pl.pallas_call(
        _mlp_kernel,
        grid=(m // tm, d_ff // tf),
        in_specs=[row,
                  pl.BlockSpec((1, d), lambda i, f: (0, 0)),
                  pl.BlockSpec((d, tf), lambda i, f: (0, f)),
                  pl.BlockSpec((tf, d), lambda i, f: (f, 0))],
        out_specs=row,
        out_shape=jax.ShapeDtypeStruct((m, d), F32),
        scratch_shapes=[pltpu.VMEM((tm, d), BF16)],
        compiler_params=pltpu.CompilerParams(
            dimension_semantics=("arbitrary", "arbitrary"), vmem_limit_bytes=VMEM_LIMIT),
        name="mlp",
    )(h, g_mlp, w1, w2)


def _tiles(seq):
    tm = min(1024, seq)
    return dict(tm_in=tm, tm_out=min(512, seq), tm_mlp=tm, tf=512, tq=min(256, seq))


def _permute_in_proj(w):
    d, n = w.shape
    n_chunks = n // (6 * COL_CHUNK)
    w = w.reshape(d, 2, 3, n_chunks, COL_CHUNK).transpose(0, 1, 3, 2, 4)
    return w.reshape(d, n).astype(BF16)


def kernel(x, meta_tokens, g_mix, w_in, conv_w, g_q, g_k, g_conv_out, g_attn_out, w_out,
           g_mlp, w_mlp_in, w_mlp_out):
    b, seq, d = x.shape
    depth = w_in.shape[0]
    n_meta = meta_tokens.shape[0]
    width = conv_w.shape[2]
    assert n_meta <= META_ROWS and seq % META_ROWS == 0
    cfg = _tiles(seq)
    meta_pad = META_ROWS - n_meta

    h = x.reshape(b * seq, d)
    hm = jnp.pad(meta_tokens.astype(x.dtype), ((meta_pad, 0), (0, 0)))
    zero_tail = jnp.zeros((width // COL_CHUNK, SUBLANES, COL_CHUNK), F32)

    for i in range(depth):
        w_in_i = _permute_in_proj(w_in[i])
        w_out_i = w_out[i].astype(BF16)
        w1_i = w_mlp_in[i].astype(BF16)
        w2_i = w_mlp_out[i].astype(BF16)
        g_mix_i = g_mix[i].reshape(1, d)
        g_co_i = g_conv_out[i].reshape(1, width)
        g_ao_i = g_attn_out[i].reshape(1, width)
        g_mlp_i = g_mlp[i].reshape(1, d)
        g_qk_i = jnp.stack([g_q[i] * (HEAD_DIM ** -0.5), g_k[i]])

        ycm, qm, km, vm, tail_m = _in_proj(hm, g_mix_i, w_in_i, conv_w[i], g_co_i, g_qk_i,
                                           zero_tail, tm=META_ROWS, tiles_per_seq=1)
        yc, q, k, v, _ = _in_proj(h, g_mix_i, w_in_i, conv_w[i], g_co_i, g_qk_i, tail_m[0],
                                  tm=cfg["tm_in"], tiles_per_seq=seq // cfg["tm_in"])
        ya = _attention(q.reshape(b, seq, width), k.reshape(b, seq, width),
                        v.reshape(b, seq, width), (km[None], vm[None]), g_ao_i,
                        tq=cfg["tq"], own_pad=0, ctx_pad=meta_pad)
        h = _out_proj(h, yc, ya.reshape(b * seq, width), w_out_i, tm=cfg["tm_out"])
        h = _mlp(h, g_mlp_i, w1_i, w2_i, tm=cfg["tm_mlp"], tf=cfg["tf"])

        if i + 1 < depth:
            yam = _attention(qm[None], km[None], vm[None], None, g_ao_i,
                             tq=META_ROWS, own_pad=meta_pad, ctx_pad=None)
            hm = _out_proj(hm, ycm, yam[0], w_out_i, tm=META_ROWS)
            hm = _mlp(hm, g_mlp_i, w1_i, w2_i, tm=META_ROWS, tf=cfg["tf"])

    return h.reshape(b, seq, d)
```

```python
import functools

import jax
import jax.numpy as jnp
from jax import lax
from jax.experimental import pallas as pl
from jax.experimental.pallas import tpu as pltpu

F32 = jnp.float32
BF16 = jnp.bfloat16

EPS = 1e-6
HEAD_DIM = 128
CONV_K = 3
SUBLANES = 8
META_ROWS = 128
COL_CHUNK = 2 * HEAD_DIM
VMEM_LIMIT = 56 * 1024 * 1024
NEG_BIG = -1e30
LOG2E = 1.4426950408889634


def _rms(x, g):
    ms = jnp.mean(x * x, axis=-1, keepdims=True)
    return x * lax.rsqrt(ms + EPS) * g


def _in_proj_kernel(x_ref, g_ref, w_ref, cw_ref, gco_ref, gqk_ref, ctx_ref,
                    yc_ref, q_ref, k_ref, v_ref, tail_ref,
                    hn_ref, cu_ref, carry_ref, *, tm, n_conv_steps, tiles_per_seq):
    i = pl.program_id(0)
    j = pl.program_id(1)
    cc = COL_CHUNK

    @pl.when(j == 0)
    def _():
        hn_ref[...] = _rms(x_ref[...], g_ref[...]).astype(BF16)

    acc = jnp.dot(hn_ref[...], w_ref[...], preferred_element_type=F32)

    @pl.when(j < n_conv_steps)
    def _conv():
        gate_b = acc[:, 0:cc]
        cu = acc[:, cc:2 * cc] * acc[:, 2 * cc:3 * cc]
        cu_ref[SUBLANES:SUBLANES + tm, :] = cu
        seq_start = (i % tiles_per_seq) == 0

        @pl.when(seq_start)
        def _():
            cu_ref[0:SUBLANES, :] = ctx_ref[0]

        @pl.when(jnp.logical_not(seq_start))
        def _():
            cu_ref[0:SUBLANES, :] = carry_ref[j]

        tail = cu[tm - SUBLANES:tm, :]
        carry_ref[j] = tail
        tail_ref[0, 0] = tail
        c1 = cu_ref[SUBLANES - 1:SUBLANES - 1 + tm, :]
        c2 = cu_ref[SUBLANES - 2:SUBLANES - 2 + tm, :]
        w = cw_ref[...]
        y = gate_b * (w[0:1, :] * c2 + w[1:2, :] * c1 + w[2:3, :] * cu)
        gco = gco_ref[...]
        for s in range(cc // HEAD_DIM):
            sl = slice(s * HEAD_DIM, (s + 1) * HEAD_DIM)
            yc_ref[:, sl] = _rms(y[:, sl], gco[:, sl]).astype(BF16)

    @pl.when(j >= n_conv_steps)
    def _qkv():
        gqk = gqk_ref[...]
        for s in range(cc // HEAD_DIM):
            sl = slice(s * HEAD_DIM, (s + 1) * HEAD_DIM)
            q_ref[:, sl] = _rms(acc[:, sl], gqk[0:1, :]).astype(BF16)
            k_ref[:, sl] = _rms(acc[:, cc + s * HEAD_DIM:cc + (s + 1) * HEAD_DIM],
                                gqk[1:2, :]).astype(BF16)
        v_ref[...] = acc[:, 2 * cc:3 * cc].astype(BF16)


def _in_proj(h, g_mix, w_perm, conv_w, g_co, g_qk, ctx_tail, *, tm, tiles_per_seq):
    m, d = h.shape
    cc = COL_CHUNK
    width = conv_w.shape[1]
    n_conv = width // cc
    n_steps = w_perm.shape[1] // (3 * cc)
    assert n_steps == 2 * n_conv and m % tm == 0

    def conv_col(i, j):
        return (0, jnp.minimum(j, n_conv - 1))

    def conv_out(i, j):
        return (i, jnp.minimum(j, n_conv - 1))

    def qkv_out(i, j):
        return (i, jnp.maximum(j - n_conv, 0))

    kern = functools.partial(_in_proj_kernel, tm=tm, n_conv_steps=n_conv,
                             tiles_per_seq=tiles_per_seq)
    act = jax.ShapeDtypeStruct((m, width), BF16)
    return pl.pallas_call(
        kern,
        grid=(m // tm, n_steps),
        in_specs=[
            pl.BlockSpec((tm, d), lambda i, j: (i, 0)),
            pl.BlockSpec((1, d), lambda i, j: (0, 0)),
            pl.BlockSpec((d, 3 * cc), lambda i, j: (0, j)),
            pl.BlockSpec((CONV_K, cc), conv_col),
            pl.BlockSpec((1, cc), conv_col),
            pl.BlockSpec((2, HEAD_DIM), lambda i, j: (0, 0)),
            pl.BlockSpec((1, SUBLANES, cc), lambda i, j: (jnp.minimum(j, n_conv - 1), 0, 0)),
        ],
        out_specs=[
            pl.BlockSpec((tm, cc), conv_out),
            pl.BlockSpec((tm, cc), qkv_out),
            pl.BlockSpec((tm, cc), qkv_out),
            pl.BlockSpec((tm, cc), qkv_out),
            pl.BlockSpec((1, 1, SUBLANES, cc),
                         lambda i, j: (i, jnp.minimum(j, n_conv - 1), 0, 0)),
        ],
        out_shape=[act, act, act, act,
                   jax.ShapeDtypeStruct((m // tm, n_conv, SUBLANES, cc), F32)],
        scratch_shapes=[
            pltpu.VMEM((tm, d), BF16),
            pltpu.VMEM((tm + SUBLANES, cc), F32),
            pltpu.VMEM((n_conv, SUBLANES, cc), F32),
        ],
        compiler_params=pltpu.CompilerParams(
            dimension_semantics=("arbitrary", "arbitrary"),
            vmem_limit_bytes=VMEM_LIMIT),
        name="in_proj",
    )(h, g_mix, w_perm, conv_w, g_co, g_qk, ctx_tail)


def _neg_abs(x):
    bits = lax.bitcast_convert_type(x, jnp.int32) | jnp.int32(-2 ** 31)
    return lax.bitcast_convert_type(bits, F32)


def _neg_suffix_matrix(n):
    rr = lax.broadcasted_iota(jnp.int32, (2 * n, n), 0)
    cc = lax.broadcasted_iota(jnp.int32, (2 * n, n), 1)
    rr = jnp.where(rr >= n, rr - n, rr)
    return jnp.where(rr >= cc, -1.0, 0.0).astype(BF16)


def _attn_kernel(*refs, seq, tq, own_pad, ctx_pad):
    has_ctx = ctx_pad is not None
    if has_ctx:
        (q_ref, k_ref, v_ref, kc_ref, vc_ref, g_ref, o_ref,
         uu_ref, uc_ref, acc_ref, r_ref) = refs
    else:
        q_ref, k_ref, v_ref, g_ref, o_ref, uu_ref, acc_ref, r_ref = refs
    tk = tq
    nq = seq // tq
    heads = q_ref.shape[2] // HEAD_DIM
    nt = (((1,), (1,)), ((), ()))

    uu_ref[...] = _neg_suffix_matrix(tk)
    if has_ctx:
        uc_ref[...] = _neg_suffix_matrix(kc_ref.shape[1])
    rr = lax.broadcasted_iota(jnp.int32, (tq, tk), 0)
    cc = lax.broadcasted_iota(jnp.int32, (tq, tk), 1)
    diag_mask = cc < rr
    if own_pad:
        diag_mask = jnp.logical_and(diag_mask, cc >= own_pad)

    def hcols(hh):
        return slice(hh * HEAD_DIM, (hh + 1) * HEAD_DIM)

    def block(qs, kbs, vbs, uu, mask):
        width = kbs[0].shape[0]
        hs = range(heads)
        zs = [lax.dot_general(qs[hh], kbs[hh], nt, preferred_element_type=F32) for hh in hs]
        css = []
        for hh in hs:
            z = zs[hh]
            nl = jnp.maximum(z, 0.0) + jnp.log2(1.0 + jnp.exp2(_neg_abs(z)))
            if mask is not None:
                nl = jnp.where(mask, nl, 0.0)
            hi = nl.astype(BF16)
            lo = (nl - hi.astype(F32)).astype(BF16)
            css.append(jnp.dot(jnp.concatenate([hi, lo], axis=1), uu,
                               preferred_element_type=F32))
        avs = []
        for hh in hs:
            r = r_ref[hh]
            t = zs[hh] + css[hh] + jnp.concatenate([r] * (width // HEAD_DIM), axis=1)
            if mask is not None:
                t = jnp.where(mask, t, NEG_BIG)
            a = jnp.exp2(t).astype(BF16)
            avs.append(jnp.dot(a, vbs[hh], preferred_element_type=F32))
            r_ref[hh] = r + jnp.broadcast_to(css[hh][:, 0:1], (tq, HEAD_DIM))
        for hh in hs:
            acc_ref[hh] += avs[hh]

    def q_block(qi):
        row0 = qi * tq if isinstance(qi, int) else pl.multiple_of(qi * tq, tq)
        qs = [q_ref[0, pl.ds(row0, tq), hcols(hh)] for hh in range(heads)]
        acc_ref[...] = jnp.zeros_like(acc_ref)
        r_ref[...] = jnp.zeros_like(r_ref)
        uu = uu_ref[...]

        def kv_blocks(c0):
            return ([k_ref[0, pl.ds(c0, tk), hcols(hh)] for hh in range(heads)],
                    [v_ref[0, pl.ds(c0, tk), hcols(hh)] for hh in range(heads)])

        block(qs, *kv_blocks(row0), uu, diag_mask)

        if nq > 1:
            def kv_step(n, carry):
                block(qs, *kv_blocks(pl.multiple_of((qi - 1 - n) * tk, tk)), uu, None)
                return carry
            lax.fori_loop(0, qi, kv_step, 0)

        if has_ctx:
            cw = kc_ref.shape[1]
            ctx_mask = lax.broadcasted_iota(jnp.int32, (tq, cw), 1) >= ctx_pad
            block(qs, [kc_ref[0, :, hcols(hh)] for hh in range(heads)],
                  [vc_ref[0, :, hcols(hh)] for hh in range(heads)], uc_ref[...], ctx_mask)

        g = g_ref[...]
        for hh in range(heads):
            o_ref[0, pl.ds(row0, tq), hcols(hh)] = _rms(acc_ref[hh], g[:, hcols(hh)]).astype(BF16)

    if nq == 1:
        q_block(0)
    else:
        def q_step(qi, carry):
            q_block(qi)
            return carry
        lax.fori_loop(0, nq, q_step, 0)


def _attention(q, k, v, ctx, g_ao, *, tq, heads, own_pad, ctx_pad):
    b, seq, width = q.shape
    hw = heads * HEAD_DIM
    blk = pl.BlockSpec((1, seq, hw), lambda bi, hi: (bi, 0, hi))
    in_specs = [blk, blk, blk]
    args = [q, k, v]
    scratch = [pltpu.VMEM((2 * tq, tq), BF16)]
    if ctx is not None:
        kc, vc = ctx
        cblk = pl.BlockSpec((1, kc.shape[1], hw), lambda bi, hi: (0, 0, hi))
        in_specs += [cblk, cblk]
        args += [kc, vc]
        scratch.append(pltpu.VMEM((2 * kc.shape[1], kc.shape[1]), BF16))
    in_specs.append(pl.BlockSpec((1, hw), lambda bi, hi: (0, hi)))
    args.append(g_ao)
    kern = functools.partial(_attn_kernel, seq=seq, tq=tq, own_pad=own_pad,
                             ctx_pad=ctx_pad if ctx is not None else None)
    return pl.pallas_call(
        kern,
        grid=(b, width // hw),
        in_specs=in_specs,
        out_specs=blk,
        out_shape=jax.ShapeDtypeStruct((b, seq, width), BF16),
        scratch_shapes=scratch + [
            pltpu.VMEM((heads, tq, HEAD_DIM), F32),
            pltpu.VMEM((heads, tq, HEAD_DIM), F32),
        ],
        compiler_params=pltpu.CompilerParams(
            dimension_semantics=("arbitrary", "arbitrary"),
            vmem_limit_bytes=VMEM_LIMIT),
        name="sb_attention",
    )(*args)


def _out_proj_kernel(h_ref, yc_ref, ya_ref, w_ref, o_ref):
    half = yc_ref.shape[1]
    o_ref[...] = (h_ref[...]
                  + jnp.dot(yc_ref[...], w_ref[0:half, :], preferred_element_type=F32)
                  + jnp.dot(ya_ref[...], w_ref[half:, :], preferred_element_type=F32))


def _out_proj(h, yc, ya, w_out, *, tm):
    m, d = h.shape
    half = yc.shape[1]
    row = pl.BlockSpec((tm, d), lambda i: (i, 0))
    yrow = pl.BlockSpec((tm, half), lambda i: (i, 0))
    return pl.pallas_call(
        _out_proj_kernel,
        grid=(m // tm,),
        in_specs=[row, yrow, yrow, pl.BlockSpec(w_out.shape, lambda i: (0, 0))],
        out_specs=row,
        out_shape=jax.ShapeDtypeStruct((m, d), F32),
        compiler_params=pltpu.CompilerParams(
            dimension_semantics=("arbitrary",), vmem_limit_bytes=VMEM_LIMIT),
        name="out_proj",
    )(h, yc, ya, w_out)


def _mlp_kernel(h_ref, g_ref, w1_ref, w2_ref, o_ref, hn_ref):
    f = pl.program_id(1)

    @pl.when(f == 0)
    def _():
        x = h_ref[...]
        hn_ref[...] = _rms(x, g_ref[...]).astype(BF16)
        o_ref[...] = x

    hid = jnp.dot(hn_ref[...], w1_ref[...], preferred_element_type=F32)
    hid = jnp.square(jnp.maximum(hid, 0.0)).astype(BF16)
    o_ref[...] += jnp.dot(hid, w2_ref[...], preferred_element_type=F32)


def _mlp(h, g_mlp, w1, w2, *, tm, tf):
    m, d = h.shape
    d_ff = w1.shape[1]
    row = pl.BlockSpec((tm, d), lambda i, f: (i, 0))
    return pl.pallas_call(
        _mlp_kernel,
        grid=(m // tm, d_ff // tf),
        in_specs=[row,
                  pl.BlockSpec((1, d), lambda i, f: (0, 0)),
                  pl.BlockSpec((d, tf), lambda i, f: (0, f)),
                  pl.BlockSpec((tf, d), lambda i, f: (f, 0))],
        out_specs=row,
        out_shape=jax.ShapeDtypeStruct((m, d), F32),
        scratch_shapes=[pltpu.VMEM((tm, d), BF16)],
        compiler_params=pltpu.CompilerParams(
            dimension_semantics=("arbitrary", "arbitrary"), vmem_limit_bytes=VMEM_LIMIT),
        name="mlp",
    )(h, g_mlp, w1, w2)


def _tiles(seq):
    tm = min(1024, seq)
    return dict(tm_in=tm, tm_out=min(512, seq), tm_mlp=tm, tf=512, tq=min(256, seq), heads=4)


def _permute_in_proj(w):
    d, n = w.shape
    n_chunks = n // (6 * COL_CHUNK)
    w = w.reshape(d, 2, 3, n_chunks, COL_CHUNK).transpose(0, 1, 3, 2, 4)
    return w.reshape(d, n).astype(BF16)


def kernel(x, meta_tokens, g_mix, w_in, conv_w, g_q, g_k, g_conv_out, g_attn_out, w_out,
           g_mlp, w_mlp_in, w_mlp_out):
    b, seq, d = x.shape
    depth = w_in.shape[0]
    n_meta = meta_tokens.shape[0]
    width = conv_w.shape[2]
    assert n_meta <= META_ROWS and seq % META_ROWS == 0
    cfg = _tiles(seq)
    meta_pad = META_ROWS - n_meta

    h = x.reshape(b * seq, d)
    hm = jnp.pad(meta_tokens.astype(x.dtype), ((meta_pad, 0), (0, 0)))
    zero_tail = jnp.zeros((width // COL_CHUNK, SUBLANES, COL_CHUNK), F32)

    for i in range(depth):
        w_in_i = _permute_in_proj(w_in[i])
        w_out_i = w_out[i].astype(BF16)
        w1_i = w_mlp_in[i].astype(BF16)
        w2_i = w_mlp_out[i].astype(BF16)
        g_mix_i = g_mix[i].reshape(1, d)
        g_co_i = g_conv_out[i].reshape(1, width)
        g_ao_i = g_attn_out[i].reshape(1, width)
        g_mlp_i = g_mlp[i].reshape(1, d)
        g_qk_i = jnp.stack([g_q[i] * (HEAD_DIM ** -0.5 * LOG2E), g_k[i]])

        ycm, qm, km, vm, tail_m = _in_proj(hm, g_mix_i, w_in_i, conv_w[i], g_co_i, g_qk_i,
                                           zero_tail, tm=META_ROWS, tiles_per_seq=1)
        yc, q, k, v, _ = _in_proj(h, g_mix_i, w_in_i, conv_w[i], g_co_i, g_qk_i, tail_m[0],
                                  tm=cfg["tm_in"], tiles_per_seq=seq // cfg["tm_in"])
        ya = _attention(q.reshape(b, seq, width), k.reshape(b, seq, width),
                        v.reshape(b, seq, width), (km[None], vm[None]), g_ao_i,
                        tq=cfg["tq"], heads=cfg["heads"], own_pad=0, ctx_pad=meta_pad)
        h = _out_proj(h, yc, ya.reshape(b * seq, width), w_out_i, tm=cfg["tm_out"])
        h = _mlp(h, g_mlp_i, w1_i, w2_i, tm=cfg["tm_mlp"], tf=cfg["tf"])

        if i + 1 < depth:
            yam = _attention(qm[None], km[None], vm[None], None, g_ao_i,
                             tq=META_ROWS, heads=cfg["heads"], own_pad=meta_pad, ctx_pad=None)
            hm = _out_proj(hm, ycm, yam[0], w_out_i, tm=META_ROWS)
            hm = _mlp(hm, g_mlp_i, w1_i, w2_i, tm=META_ROWS, tf=cfg["tf"])

    return h.reshape(b, seq, d)
```

```python
import functools

import jax
import jax.numpy as jnp
from jax import lax
from jax.experimental import pallas as pl
from jax.experimental.pallas import tpu as pltpu

F32 = jnp.float32
BF16 = jnp.bfloat16

EPS = 1e-6
HEAD_DIM = 128
CONV_K = 3
SUBLANES = 8
META_ROWS = 128
COL_CHUNK = 2 * HEAD_DIM
VMEM_LIMIT = 56 * 1024 * 1024
NEG_BIG = -1e30
LOG2E = 1.4426950408889634
DEAD_LOG2 = -160.0


def _rms(x, g):
    ms = jnp.mean(x * x, axis=-1, keepdims=True)
    return x * lax.rsqrt(ms + EPS) * g


def _in_proj_kernel(x_ref, g_ref, w_ref, cw_ref, gco_ref, gqk_ref, ctx_ref,
                    yc_ref, q_ref, k_ref, v_ref, tail_ref,
                    hn_ref, cu_ref, carry_ref, *, tm, n_conv_steps, tiles_per_seq):
    i = pl.program_id(0)
    j = pl.program_id(1)
    cc = COL_CHUNK

    @pl.when(j == 0)
    def _():
        hn_ref[...] = _rms(x_ref[...], g_ref[...]).astype(BF16)

    acc = jnp.dot(hn_ref[...], w_ref[...], preferred_element_type=F32)

    @pl.when(j < n_conv_steps)
    def _conv():
        gate_b = acc[:, 0:cc]
        cu = acc[:, cc:2 * cc] * acc[:, 2 * cc:3 * cc]
        cu_ref[SUBLANES:SUBLANES + tm, :] = cu
        seq_start = (i % tiles_per_seq) == 0

        @pl.when(seq_start)
        def _():
            cu_ref[0:SUBLANES, :] = ctx_ref[0]

        @pl.when(jnp.logical_not(seq_start))
        def _():
            cu_ref[0:SUBLANES, :] = carry_ref[j]

        tail = cu[tm - SUBLANES:tm, :]
        carry_ref[j] = tail
        tail_ref[0, 0] = tail
        c1 = cu_ref[SUBLANES - 1:SUBLANES - 1 + tm, :]
        c2 = cu_ref[SUBLANES - 2:SUBLANES - 2 + tm, :]
        w = cw_ref[...]
        y = gate_b * (w[0:1, :] * c2 + w[1:2, :] * c1 + w[2:3, :] * cu)
        gco = gco_ref[...]
        for s in range(cc // HEAD_DIM):
            sl = slice(s * HEAD_DIM, (s + 1) * HEAD_DIM)
            yc_ref[:, sl] = _rms(y[:, sl], gco[:, sl]).astype(BF16)

    @pl.when(j >= n_conv_steps)
    def _qkv():
        gqk = gqk_ref[...]
        for s in range(cc // HEAD_DIM):
            sl = slice(s * HEAD_DIM, (s + 1) * HEAD_DIM)
            q_ref[:, sl] = _rms(acc[:, sl], gqk[0:1, :]).astype(BF16)
            k_ref[:, sl] = _rms(acc[:, cc + s * HEAD_DIM:cc + (s + 1) * HEAD_DIM],
                                gqk[1:2, :]).astype(BF16)
        v_ref[...] = acc[:, 2 * cc:3 * cc].astype(BF16)


def _in_proj(h, g_mix, w_perm, conv_w, g_co, g_qk, ctx_tail, *, tm, tiles_per_seq):
    m, d = h.shape
    cc = COL_CHUNK
    width = conv_w.shape[1]
    n_conv = width // cc
    n_steps = w_perm.shape[1] // (3 * cc)
    assert n_steps == 2 * n_conv and m % tm == 0

    def conv_col(i, j):
        return (0, jnp.minimum(j, n_conv - 1))

    def conv_out(i, j):
        return (i, jnp.minimum(j, n_conv - 1))

    def qkv_out(i, j):
        return (i, jnp.maximum(j - n_conv, 0))

    kern = functools.partial(_in_proj_kernel, tm=tm, n_conv_steps=n_conv,
                             tiles_per_seq=tiles_per_seq)
    act = jax.ShapeDtypeStruct((m, width), BF16)
    return pl.pallas_call(
        kern,
        grid=(m // tm, n_steps),
        in_specs=[
            pl.BlockSpec((tm, d), lambda i, j: (i, 0)),
            pl.BlockSpec((1, d), lambda i, j: (0, 0)),
            pl.BlockSpec((d, 3 * cc), lambda i, j: (0, j)),
            pl.BlockSpec((CONV_K, cc), conv_col),
            pl.BlockSpec((1, cc), conv_col),
            pl.BlockSpec((2, HEAD_DIM), lambda i, j: (0, 0)),
            pl.BlockSpec((1, SUBLANES, cc), lambda i, j: (jnp.minimum(j, n_conv - 1), 0, 0)),
        ],
        out_specs=[
            pl.BlockSpec((tm, cc), conv_out),
            pl.BlockSpec((tm, cc), qkv_out),
            pl.BlockSpec((tm, cc), qkv_out),
            pl.BlockSpec((tm, cc), qkv_out),
            pl.BlockSpec((1, 1, SUBLANES, cc),
                         lambda i, j: (i, jnp.minimum(j, n_conv - 1), 0, 0)),
        ],
        out_shape=[act, act, act, act,
                   jax.ShapeDtypeStruct((m // tm, n_conv, SUBLANES, cc), F32)],
        scratch_shapes=[
            pltpu.VMEM((tm, d), BF16),
            pltpu.VMEM((tm + SUBLANES, cc), F32),
            pltpu.VMEM((n_conv, SUBLANES, cc), F32),
        ],
        compiler_params=pltpu.CompilerParams(
            dimension_semantics=("arbitrary", "arbitrary"),
            vmem_limit_bytes=VMEM_LIMIT),
        name="in_proj",
    )(h, g_mix, w_perm, conv_w, g_co, g_qk, ctx_tail)


def _neg_abs(x):
    bits = lax.bitcast_convert_type(x, jnp.int32) | jnp.int32(-2 ** 31)
    return lax.bitcast_convert_type(bits, F32)


def _neg_suffix_matrix(n):
    rr = lax.broadcasted_iota(jnp.int32, (2 * n, n), 0)
    cc = lax.broadcasted_iota(jnp.int32, (2 * n, n), 1)
    rr = jnp.where(rr >= n, rr - n, rr)
    return jnp.where(rr >= cc, -1.0, 0.0).astype(BF16)


def _attn_kernel(*refs, seq, tq, own_pad, ctx_pad):
    has_ctx = ctx_pad is not None
    if has_ctx:
        (q_ref, k_ref, v_ref, kc_ref, vc_ref, g_ref, o_ref,
         uu_ref, uc_ref, acc_ref, r_ref) = refs
    else:
        q_ref, k_ref, v_ref, g_ref, o_ref, uu_ref, acc_ref, r_ref = refs
    tk = tq
    nq = seq // tq
    heads = q_ref.shape[2] // HEAD_DIM
    nt = (((1,), (1,)), ((), ()))

    uu_ref[...] = _neg_suffix_matrix(tk)
    if has_ctx:
        uc_ref[...] = _neg_suffix_matrix(kc_ref.shape[1])
    rr = lax.broadcasted_iota(jnp.int32, (tq, tk), 0)
    cc = lax.broadcasted_iota(jnp.int32, (tq, tk), 1)
    diag_mask = cc < rr
    if own_pad:
        diag_mask = jnp.logical_and(diag_mask, cc >= own_pad)

    def hcols(hh):
        return slice(hh * HEAD_DIM, (hh + 1) * HEAD_DIM)

    def block(qs, kbs, vbs, uu, mask):
        width = kbs[0].shape[0]
        hs = range(heads)
        zs = [lax.dot_general(qs[hh], kbs[hh], nt, preferred_element_type=F32) for hh in hs]
        css = []
        for hh in hs:
            z = zs[hh]
            nl = jnp.maximum(z, 0.0) + jnp.log2(1.0 + jnp.exp2(_neg_abs(z)))
            if mask is not None:
                nl = jnp.where(mask, nl, 0.0)
            hi = nl.astype(BF16)
            lo = (nl - hi.astype(F32)).astype(BF16)
            css.append(jnp.dot(jnp.concatenate([hi, lo], axis=1), uu,
                               preferred_element_type=F32))
        avs = []
        for hh in hs:
            r = r_ref[hh]
            t = zs[hh] + css[hh] + jnp.concatenate([r] * (width // HEAD_DIM), axis=1)
            if mask is not None:
                t = jnp.where(mask, t, NEG_BIG)
            a = jnp.exp2(t).astype(BF16)
            avs.append(jnp.dot(a, vbs[hh], preferred_element_type=F32))
            r_ref[hh] = r + jnp.broadcast_to(css[hh][:, 0:1], (tq, HEAD_DIM))
        for hh in hs:
            acc_ref[hh] += avs[hh]

    def q_block(qi):
        row0 = qi * tq if isinstance(qi, int) else pl.multiple_of(qi * tq, tq)
        qs = [q_ref[0, pl.ds(row0, tq), hcols(hh)] for hh in range(heads)]
        acc_ref[...] = jnp.zeros_like(acc_ref)
        r_ref[...] = jnp.zeros_like(r_ref)
        uu = uu_ref[...]

        def kv_blocks(c0):
            return ([k_ref[0, pl.ds(c0, tk), hcols(hh)] for hh in range(heads)],
                    [v_ref[0, pl.ds(c0, tk), hcols(hh)] for hh in range(heads)])

        block(qs, *kv_blocks(row0), uu, diag_mask)

        def r_max():
            return jnp.max(r_ref[...])

        alive = r_max() > DEAD_LOG2
        if nq > 1:
            def kv_cond(carry):
                return jnp.logical_and(carry[0] < qi, carry[1])

            def kv_step(carry):
                n = carry[0]
                block(qs, *kv_blocks(pl.multiple_of((qi - 1 - n) * tk, tk)), uu, None)
                return n + 1, r_max() > DEAD_LOG2

            _, alive = lax.while_loop(kv_cond, kv_step, (jnp.int32(0), alive))

        if has_ctx:
            @pl.when(alive)
            def _():
                cw = kc_ref.shape[1]
                ctx_mask = lax.broadcasted_iota(jnp.int32, (tq, cw), 1) >= ctx_pad
                block(qs, [kc_ref[0, :, hcols(hh)] for hh in range(heads)],
                      [vc_ref[0, :, hcols(hh)] for hh in range(heads)], uc_ref[...], ctx_mask)

        g = g_ref[...]
        for hh in range(heads):
            o_ref[0, pl.ds(row0, tq), hcols(hh)] = _rms(acc_ref[hh], g[:, hcols(hh)]).astype(BF16)

    if nq == 1:
        q_block(0)
    else:
        def q_step(qi, carry):
            q_block(qi)
            return carry
        lax.fori_loop(0, nq, q_step, 0)


def _attention(q, k, v, ctx, g_ao, *, tq, heads, own_pad, ctx_pad):
    b, seq, width = q.shape
    hw = heads * HEAD_DIM
    blk = pl.BlockSpec((1, seq, hw), lambda bi, hi: (bi, 0, hi))
    in_specs = [blk, blk, blk]
    args = [q, k, v]
    scratch = [pltpu.VMEM((2 * tq, tq), BF16)]
    if ctx is not None:
        kc, vc = ctx
        cblk = pl.BlockSpec((1, kc.shape[1], hw), lambda bi, hi: (0, 0, hi))
        in_specs += [cblk, cblk]
        args += [kc, vc]
        scratch.append(pltpu.VMEM((2 * kc.shape[1], kc.shape[1]), BF16))
    in_specs.append(pl.BlockSpec((1, hw), lambda bi, hi: (0, hi)))
    args.append(g_ao)
    kern = functools.partial(_attn_kernel, seq=seq, tq=tq, own_pad=own_pad,
                             ctx_pad=ctx_pad if ctx is not None else None)
    return pl.pallas_call(
        kern,
        grid=(b, width // hw),
        in_specs=in_specs,
        out_specs=blk,
        out_shape=jax.ShapeDtypeStruct((b, seq, width), BF16),
        scratch_shapes=scratch + [
            pltpu.VMEM((heads, tq, HEAD_DIM), F32),
            pltpu.VMEM((heads, tq, HEAD_DIM), F32),
        ],
        compiler_params=pltpu.CompilerParams(
            dimension_semantics=("arbitrary", "arbitrary"),
            vmem_limit_bytes=VMEM_LIMIT),
        name="sb_attention",
    )(*args)


def _out_proj_kernel(h_ref, yc_ref, ya_ref, w_ref, o_ref):
    half = yc_ref.shape[1]
    o_ref[...] = (h_ref[...]
                  + jnp.dot(yc_ref[...], w_ref[0:half, :], preferred_element_type=F32)
                  + jnp.dot(ya_ref[...], w_ref[half:, :], preferred_element_type=F32))


def _out_proj(h, yc, ya, w_out, *, tm):
    m, d = h.shape
    half = yc.shape[1]
    row = pl.BlockSpec((tm, d), lambda i: (i, 0))
    yrow = pl.BlockSpec((tm, half), lambda i: (i, 0))
    return pl.pallas_call(
        _out_proj_kernel,
        grid=(m // tm,),
        in_specs=[row, yrow, yrow, pl.BlockSpec(w_out.shape, lambda i: (0, 0))],
        out_specs=row,
        out_shape=jax.ShapeDtypeStruct((m, d), F32),
        compiler_params=pltpu.CompilerParams(
            dimension_semantics=("arbitrary",), vmem_limit_bytes=VMEM_LIMIT),
        name="out_proj",
    )(h, yc, ya, w_out)


def _mlp_kernel(h_ref, g_ref, w1_ref, w2_ref, o_ref, hn_ref):
    f = pl.program_id(1)

    @pl.when(f == 0)
    def _():
        x = h_ref[...]
        hn_ref[...] = _rms(x, g_ref[...]).astype(BF16)
        o_ref[...] = x

    hid = jnp.dot(hn_ref[...], w1_ref[...], preferred_element_type=F32)
    hid = jnp.square(jnp.maximum(hid, 0.0)).astype(BF16)
    o_ref[...] += jnp.dot(hid, w2_ref[...], preferred_element_type=F32)


def _mlp(h, g_mlp, w1, w2, *, tm, tf):
    m, d = h.shape
    d_ff = w1.shape[1]
    row = pl.BlockSpec((tm, d), lambda i, f: (i, 0))
    return pl.pallas_call(
        _mlp_kernel,
        grid=(m // tm, d_ff // tf),
        in_specs=[row,
                  pl.BlockSpec((1, d), lambda i, f: (0, 0)),
                  pl.BlockSpec((d, tf), lambda i, f: (0, f)),
                  pl.BlockSpec((tf, d), lambda i, f: (f, 0))],
        out_specs=row,
        out_shape=jax.ShapeDtypeStruct((m, d), F32),
        scratch_shapes=[pltpu.VMEM((tm, d), BF16)],
        compiler_params=pltpu.CompilerParams(
            dimension_semantics=("arbitrary", "arbitrary"), vmem_limit_bytes=VMEM_LIMIT),
        name="mlp",
    )(h, g_mlp, w1, w2)


def _tiles(seq):
    tm = min(1024, seq)
    return dict(tm_in=tm, tm_out=min(512, seq), tm_mlp=tm, tf=512, tq=min(256, seq), heads=4)


def _permute_in_proj(w):
    d, n = w.shape
    n_chunks = n // (6 * COL_CHUNK)
    w = w.reshape(d, 2, 3, n_chunks, COL_CHUNK).transpose(0, 1, 3, 2, 4)
    return w.reshape(d, n).astype(BF16)


def kernel(x, meta_tokens, g_mix, w_in, conv_w, g_q, g_k, g_conv_out, g_attn_out, w_out,
           g_mlp, w_mlp_in, w_mlp_out):
    b, seq, d = x.shape
    depth = w_in.shape[0]
    n_meta = meta_tokens.shape[0]
    width = conv_w.shape[2]
    assert n_meta <= META_ROWS and seq % META_ROWS == 0
    cfg = _tiles(seq)
    meta_pad = META_ROWS - n_meta

    h = x.reshape(b * seq, d)
    hm = jnp.pad(meta_tokens.astype(x.dtype), ((meta_pad, 0), (0, 0)))
    zero_tail = jnp.zeros((width // COL_CHUNK, SUBLANES, COL_CHUNK), F32)

    for i in range(depth):
        w_in_i = _permute_in_proj(w_in[i])
        w_out_i = w_out[i].astype(BF16)
        w1_i = w_mlp_in[i].astype(BF16)
        w2_i = w_mlp_out[i].astype(BF16)
        g_mix_i = g_mix[i].reshape(1, d)
        g_co_i = g_conv_out[i].reshape(1, width)
        g_ao_i = g_attn_out[i].reshape(1, width)
        g_mlp_i = g_mlp[i].reshape(1, d)
        g_qk_i = jnp.stack([g_q[i] * (HEAD_DIM ** -0.5 * LOG2E), g_k[i]])

        ycm, qm, km, vm, tail_m = _in_proj(hm, g_mix_i, w_in_i, conv_w[i], g_co_i, g_qk_i,
                                           zero_tail, tm=META_ROWS, tiles_per_seq=1)
        yc, q, k, v, _ = _in_proj(h, g_mix_i, w_in_i, conv_w[i], g_co_i, g_qk_i, tail_m[0],
                                  tm=cfg["tm_in"], tiles_per_seq=seq // cfg["tm_in"])
        ya = _attention(q.reshape(b, seq, width), k.reshape(b, seq, width),
                        v.reshape(b, seq, width), (km[None], vm[None]), g_ao_i,
                        tq=cfg["tq"], heads=cfg["heads"], own_pad=0, ctx_pad=meta_pad)
        h = _out_proj(h, yc, ya.reshape(b * seq, width), w_out_i, tm=cfg["tm_out"])
        h = _mlp(h, g_mlp_i, w1_i, w2_i, tm=cfg["tm_mlp"], tf=cfg["tf"])

        if i + 1 < depth:
            yam = _attention(qm[None], km[None], vm[None], None, g_ao_i,
                             tq=META_ROWS, heads=cfg["heads"], own_pad=meta_pad, ctx_pad=None)
            hm = _out_proj(hm, ycm, yam[0], w_out_i, tm=META_ROWS)
            hm = _mlp(hm, g_mlp_i, w1_i, w2_i, tm=META_ROWS, tf=cfg["tf"])

    return h.reshape(b, seq, d)
```

```python
import functools

import jax
import jax.numpy as jnp
from jax import lax
from jax.experimental import pallas as pl
from jax.experimental.pallas import tpu as pltpu

F32 = jnp.float32
BF16 = jnp.bfloat16

EPS = 1e-6
HEAD_DIM = 128
CONV_K = 3
SUBLANES = 8
META_ROWS = 128
COL_CHUNK = 2 * HEAD_DIM
VMEM_LIMIT = 56 * 1024 * 1024
NEG_BIG = -1e30
LOG2E = 1.4426950408889634
DEAD_LOG2 = -160.0


def _rms(x, g):
    ms = jnp.mean(x * x, axis=-1, keepdims=True)
    return x * lax.rsqrt(ms + EPS) * g


def _in_proj_kernel(x_ref, g_ref, wa_ref, wb_ref, wc_ref, cw_ref, gco_ref, gqk_ref, ctx_ref,
                    yc_ref, q_ref, k_ref, v_ref, tail_ref,
                    hn_ref, cu_ref, carry_ref, *, tm, tiles_per_seq):
    i = pl.program_id(0)
    j = pl.program_id(1)
    cc = COL_CHUNK
    n_chunks = wa_ref.shape[1] // cc

    def proj(w_ref, c):
        return jnp.dot(hn_ref[...], w_ref[:, c * cc:(c + 1) * cc], preferred_element_type=F32)

    @pl.when(j == 0)
    def _conv():
        hn_ref[...] = _rms(x_ref[...], g_ref[...]).astype(BF16)

        @pl.when((i % tiles_per_seq) == 0)
        def _():
            carry_ref[...] = ctx_ref[...]

        for c in range(n_chunks):
            cs = slice(c * cc, (c + 1) * cc)
            gate_b = proj(wa_ref, c)
            cu = proj(wb_ref, c) * proj(wc_ref, c)
            cu_ref[c, SUBLANES:SUBLANES + tm, :] = cu
            cu_ref[c, 0:SUBLANES, :] = carry_ref[:, cs]
            tail = cu[tm - SUBLANES:tm, :]
            carry_ref[:, cs] = tail
            tail_ref[0, :, cs] = tail
            c1 = cu_ref[c, SUBLANES - 1:SUBLANES - 1 + tm, :]
            c2 = cu_ref[c, SUBLANES - 2:SUBLANES - 2 + tm, :]
            w = cw_ref[:, cs]
            y = gate_b * (w[0:1, :] * c2 + w[1:2, :] * c1 + w[2:3, :] * cu)
            for s in range(c * cc, (c + 1) * cc, HEAD_DIM):
                sl = slice(s, s + HEAD_DIM)
                yc_ref[:, sl] = _rms(y[:, s - c * cc:s - c * cc + HEAD_DIM],
                                     gco_ref[:, sl]).astype(BF16)

    @pl.when(j == 1)
    def _qkv():
        gqk = gqk_ref[...]
        for w_ref, o_ref, gain in ((wa_ref, q_ref, gqk[0:1, :]), (wb_ref, k_ref, gqk[1:2, :])):
            for c in range(n_chunks):
                acc = proj(w_ref, c)
                for s in range(0, cc, HEAD_DIM):
                    o_ref[:, c * cc + s:c * cc + s + HEAD_DIM] = _rms(
                        acc[:, s:s + HEAD_DIM], gain).astype(BF16)
        for c in range(n_chunks):
            v_ref[:, c * cc:(c + 1) * cc] = proj(wc_ref, c).astype(BF16)


def _in_proj(h, g_mix, w_all, layer, conv_w, g_co, g_qk, ctx_tail, *, tm, tiles_per_seq):
    m, d = h.shape
    width = conv_w.shape[1]
    assert w_all.shape[2] == 6 * width and width % COL_CHUNK == 0 and m % tm == 0

    def section(s):
        return pl.BlockSpec((None, d, width), lambda i, j: (layer, 0, 3 * j + s))

    kern = functools.partial(_in_proj_kernel, tm=tm, tiles_per_seq=tiles_per_seq)
    act = jax.ShapeDtypeStruct((m, width), BF16)
    row = pl.BlockSpec((tm, width), lambda i, j: (i, 0))
    const = lambda i, j: (0, 0)
    return pl.pallas_call(
        kern,
        grid=(m // tm, 2),
        in_specs=[
            pl.BlockSpec((tm, d), lambda i, j: (i, 0)),
            pl.BlockSpec((1, d), const),
            section(0), section(1), section(2),
            pl.BlockSpec((CONV_K, width), const),
            pl.BlockSpec((1, width), const),
            pl.BlockSpec((2, HEAD_DIM), const),
            pl.BlockSpec((SUBLANES, width), const),
        ],
        out_specs=[row, row, row, row,
                   pl.BlockSpec((1, SUBLANES, width), lambda i, j: (i, 0, 0))],
        out_shape=[act, act, act, act,
                   jax.ShapeDtypeStruct((m // tm, SUBLANES, width), F32)],
        scratch_shapes=[
            pltpu.VMEM((tm, d), BF16),
            pltpu.VMEM((width // COL_CHUNK, tm + SUBLANES, COL_CHUNK), F32),
            pltpu.VMEM((SUBLANES, width), F32),
        ],
        compiler_params=pltpu.CompilerParams(
            dimension_semantics=("arbitrary", "arbitrary"),
            vmem_limit_bytes=VMEM_LIMIT),
        name="in_proj",
    )(h, g_mix, w_all, w_all, w_all, conv_w, g_co, g_qk, ctx_tail)


def _neg_abs(x):
    bits = lax.bitcast_convert_type(x, jnp.int32) | jnp.int32(-2 ** 31)
    return lax.bitcast_convert_type(bits, F32)


def _neg_suffix_matrix(n):
    rr = lax.broadcasted_iota(jnp.int32, (2 * n, n), 0)
    cc = lax.broadcasted_iota(jnp.int32, (2 * n, n), 1)
    rr = jnp.where(rr >= n, rr - n, rr)
    return jnp.where(rr >= cc, -1.0, 0.0).astype(BF16)


def _attn_kernel(*refs, seq, tq, own_pad, ctx_pad):
    has_ctx = ctx_pad is not None
    if has_ctx:
        (q_ref, k_ref, v_ref, kc_ref, vc_ref, g_ref, o_ref,
         uu_ref, uc_ref, acc_ref, r_ref) = refs
    else:
        q_ref, k_ref, v_ref, g_ref, o_ref, uu_ref, acc_ref, r_ref = refs
    tk = tq
    nq = seq // tq
    heads = q_ref.shape[2] // HEAD_DIM
    nt = (((1,), (1,)), ((), ()))

    uu_ref[...] = _neg_suffix_matrix(tk)
    if has_ctx:
        uc_ref[...] = _neg_suffix_matrix(kc_ref.shape[1])
    rr = lax.broadcasted_iota(jnp.int32, (tq, tk), 0)
    cc = lax.broadcasted_iota(jnp.int32, (tq, tk), 1)
    diag_mask = cc < rr
    if own_pad:
        diag_mask = jnp.logical_and(diag_mask, cc >= own_pad)

    def hcols(hh):
        return slice(hh * HEAD_DIM, (hh + 1) * HEAD_DIM)

    def block(qs, kbs, vbs, uu, mask):
        width = kbs[0].shape[0]
        hs = range(heads)
        zs = [lax.dot_general(qs[hh], kbs[hh], nt, preferred_element_type=F32) for hh in hs]
        css = []
        for hh in hs:
            z = zs[hh]
            nl = jnp.maximum(z, 0.0) + jnp.log2(1.0 + jnp.exp2(_neg_abs(z)))
            if mask is not None:
                nl = jnp.where(mask, nl, 0.0)
            hi = nl.astype(BF16)
            lo = (nl - hi.astype(F32)).astype(BF16)
            css.append(jnp.dot(jnp.concatenate([hi, lo], axis=1), uu,
                               preferred_element_type=F32))
        avs = []
        for hh in hs:
            r = r_ref[hh]
            t = zs[hh] + css[hh] + jnp.concatenate([r] * (width // HEAD_DIM), axis=1)
            if mask is not None:
                t = jnp.where(mask, t, NEG_BIG)
            a = jnp.exp2(t).astype(BF16)
            avs.append(jnp.dot(a, vbs[hh], preferred_element_type=F32))
            r_ref[hh] = r + jnp.broadcast_to(css[hh][:, 0:1], (tq, HEAD_DIM))
        for hh in hs:
            acc_ref[hh] += avs[hh]

    def q_block(qi):
        row0 = qi * tq if isinstance(qi, int) else pl.multiple_of(qi * tq, tq)
        qs = [q_ref[0, pl.ds(row0, tq), hcols(hh)] for hh in range(heads)]
        acc_ref[...] = jnp.zeros_like(acc_ref)
        r_ref[...] = jnp.zeros_like(r_ref)
        uu = uu_ref[...]

        def kv_blocks(c0):
            return ([k_ref[0, pl.ds(c0, tk), hcols(hh)] for hh in range(heads)],
                    [v_ref[0, pl.ds(c0, tk), hcols(hh)] for hh in range(heads)])

        block(qs, *kv_blocks(row0), uu, diag_mask)

        def r_max():
            return jnp.max(r_ref[...])

        alive = r_max() > DEAD_LOG2
        if nq > 1:
            def kv_cond(carry):
                return jnp.logical_and(carry[0] < qi, carry[1])

            def kv_step(carry):
                n = carry[0]
                block(qs, *kv_blocks(pl.multiple_of((qi - 1 - n) * tk, tk)), uu, None)
                return n + 1, r_max() > DEAD_LOG2

            _, alive = lax.while_loop(kv_cond, kv_step, (jnp.int32(0), alive))

        if has_ctx:
            @pl.when(alive)
            def _():
                cw = kc_ref.shape[1]
                ctx_mask = lax.broadcasted_iota(jnp.int32, (tq, cw), 1) >= ctx_pad
                block(qs, [kc_ref[0, :, hcols(hh)] for hh in range(heads)],
                      [vc_ref[0, :, hcols(hh)] for hh in range(heads)], uc_ref[...], ctx_mask)

        g = g_ref[...]
        for hh in range(heads):
            o_ref[0, pl.ds(row0, tq), hcols(hh)] = _rms(acc_ref[hh], g[:, hcols(hh)]).astype(BF16)

    if nq == 1:
        q_block(0)
    else:
        def q_step(qi, carry):
            q_block(qi)
            return carry
        lax.fori_loop(0, nq, q_step, 0)


def _attention(q, k, v, ctx, g_ao, *, tq, heads, own_pad, ctx_pad):
    b, seq, width = q.shape
    hw = heads * HEAD_DIM
    blk = pl.BlockSpec((1, seq, hw), lambda bi, hi: (bi, 0, hi))
    in_specs = [blk, blk, blk]
    args = [q, k, v]
    scratch = [pltpu.VMEM((2 * tq, tq), BF16)]
    if ctx is not None:
        kc, vc = ctx
        cblk = pl.BlockSpec((1, kc.shape[1], hw), lambda bi, hi: (0, 0, hi))
        in_specs += [cblk, cblk]
        args += [kc, vc]
        scratch.append(pltpu.VMEM((2 * kc.shape[1], kc.shape[1]), BF16))
    in_specs.append(pl.BlockSpec((1, hw), lambda bi, hi: (0, hi)))
    args.append(g_ao)
    kern = functools.partial(_attn_kernel, seq=seq, tq=tq, own_pad=own_pad,
                             ctx_pad=ctx_pad if ctx is not None else None)
    return pl.pallas_call(
        kern,
        grid=(b, width // hw),
        in_specs=in_specs,
        out_specs=blk,
        out_shape=jax.ShapeDtypeStruct((b, seq, width), BF16),
        scratch_shapes=scratch + [
            pltpu.VMEM((heads, tq, HEAD_DIM), F32),
            pltpu.VMEM((heads, tq, HEAD_DIM), F32),
        ],
        compiler_params=pltpu.CompilerParams(
            dimension_semantics=("arbitrary", "arbitrary"),
            vmem_limit_bytes=VMEM_LIMIT),
        name="sb_attention",
    )(*args)


def _out_proj_kernel(h_ref, yc_ref, ya_ref, w_ref, o_ref):
    half = yc_ref.shape[1]
    o_ref[...] = (h_ref[...]
                  + jnp.dot(yc_ref[...], w_ref[0:half, :], preferred_element_type=F32)
                  + jnp.dot(ya_ref[...], w_ref[half:, :], preferred_element_type=F32))


def _out_proj(h, yc, ya, w_out, layer, *, tm):
    m, d = h.shape
    half = yc.shape[1]
    row = pl.BlockSpec((tm, d), lambda i: (i, 0))
    yrow = pl.BlockSpec((tm, half), lambda i: (i, 0))
    return pl.pallas_call(
        _out_proj_kernel,
        grid=(m // tm,),
        in_specs=[row, yrow, yrow,
                  pl.BlockSpec((None,) + w_out.shape[1:], lambda i: (layer, 0, 0))],
        out_specs=row,
        out_shape=jax.ShapeDtypeStruct((m, d), F32),
        compiler_params=pltpu.CompilerParams(
            dimension_semantics=("arbitrary",), vmem_limit_bytes=VMEM_LIMIT),
        name="out_proj",
    )(h, yc, ya, w_out)


def _mlp_kernel(h_ref, g_ref, w1_ref, w2_ref, o_ref, hn_ref):
    f = pl.program_id(1)

    @pl.when(f == 0)
    def _():
        x = h_ref[...]
        hn_ref[...] = _rms(x, g_ref[...]).astype(BF16)
        o_ref[...] = x

    hid = jnp.dot(hn_ref[...], w1_ref[...], preferred_element_type=F32)
    hid = jnp.square(jnp.maximum(hid, 0.0)).astype(BF16)
    o_ref[...] += jnp.dot(hid, w2_ref[...], preferred_element_type=F32)


def _mlp(h, g_mlp, w1, w2, layer, *, tm, tf):
    m, d = h.shape
    d_ff = w1.shape[2]
    row = pl.BlockSpec((tm, d), lambda i, f: (i, 0))
    return pl.pallas_call(
        _mlp_kernel,
        grid=(m // tm, d_ff // tf),
        in_specs=[row,
                  pl.BlockSpec((1, d), lambda i, f: (0, 0)),
                  pl.BlockSpec((None, d, tf), lambda i, f: (layer, 0, f)),
                  pl.BlockSpec((None, tf, d), lambda i, f: (layer, f, 0))],
        out_specs=row,
        out_shape=jax.ShapeDtypeStruct((m, d), F32),
        scratch_shapes=[pltpu.VMEM((tm, d), BF16)],
        compiler_params=pltpu.CompilerParams(
            dimension_semantics=("arbitrary", "arbitrary"), vmem_limit_bytes=VMEM_LIMIT),
        name="mlp",
    )(h, g_mlp, w1, w2)


def _tiles(seq):
    return dict(tm_in=min(512, seq), tm_out=min(512, seq), tm_mlp=min(1024, seq), tf=512,
                tq=min(256, seq), heads=4)


def kernel(x, meta_tokens, g_mix, w_in, conv_w, g_q, g_k, g_conv_out, g_attn_out, w_out,
           g_mlp, w_mlp_in, w_mlp_out):
    b, seq, d = x.shape
    depth = w_in.shape[0]
    n_meta = meta_tokens.shape[0]
    width = conv_w.shape[2]
    assert n_meta <= META_ROWS and seq % META_ROWS == 0
    cfg = _tiles(seq)
    meta_pad = META_ROWS - n_meta

    h = x.reshape(b * seq, d)
    hm = jnp.pad(meta_tokens.astype(x.dtype), ((meta_pad, 0), (0, 0)))
    zero_tail = jnp.zeros((SUBLANES, width), F32)
    w_in_b = w_in.astype(BF16)
    w_out_b = w_out.astype(BF16)
    w1_b = w_mlp_in.astype(BF16)
    w2_b = w_mlp_out.astype(BF16)

    for i in range(depth):
        g_mix_i = g_mix[i].reshape(1, d)
        g_co_i = g_conv_out[i].reshape(1, width)
        g_ao_i = g_attn_out[i].reshape(1, width)
        g_mlp_i = g_mlp[i].reshape(1, d)
        g_qk_i = jnp.stack([g_q[i] * (HEAD_DIM ** -0.5 * LOG2E), g_k[i]])

        ycm, qm, km, vm, tail_m = _in_proj(hm, g_mix_i, w_in_b, i, conv_w[i], g_co_i, g_qk_i,
                                           zero_tail, tm=META_ROWS, tiles_per_seq=1)
        yc, q, k, v, _ = _in_proj(h, g_mix_i, w_in_b, i, conv_w[i], g_co_i, g_qk_i, tail_m[0],
                                  tm=cfg["tm_in"], tiles_per_seq=seq // cfg["tm_in"])
        ya = _attention(q.reshape(b, seq, width), k.reshape(b, seq, width),
                        v.reshape(b, seq, width), (km[None], vm[None]), g_ao_i,
                        tq=cfg["tq"], heads=cfg["heads"], own_pad=0, ctx_pad=meta_pad)
        h = _out_proj(h, yc, ya.reshape(b * seq, width), w_out_b, i, tm=cfg["tm_out"])
        h = _mlp(h, g_mlp_i, w1_b, w2_b, i, tm=cfg["tm_mlp"], tf=cfg["tf"])

        if i + 1 < depth:
            yam = _attention(qm[None], km[None], vm[None], None, g_ao_i,
                             tq=META_ROWS, heads=cfg["heads"], own_pad=meta_pad, ctx_pad=None)
            hm = _out_proj(hm, ycm, yam[0], w_out_b, i, tm=META_ROWS)
            hm = _mlp(hm, g_mlp_i, w1_b, w2_b, i, tm=META_ROWS, tf=cfg["tf"])

    return h.reshape(b, seq, d)
```

```python
import functools

import jax
import jax.numpy as jnp
from jax import lax
from jax.experimental import pallas as pl
from jax.experimental.pallas import tpu as pltpu

F32 = jnp.float32
BF16 = jnp.bfloat16

EPS = 1e-6
HEAD_DIM = 128
CONV_K = 3
SUBLANES = 8
META_ROWS = 128
COL_CHUNK = 2 * HEAD_DIM
VMEM_LIMIT = 56 * 1024 * 1024
NEG_BIG = -1e30
LOG2E = 1.4426950408889634
DEAD_LOG2 = -160.0


def _rms(x, g):
    ms = jnp.mean(x * x, axis=-1, keepdims=True)
    return x * lax.rsqrt(ms + EPS) * g


def _in_proj_kernel(x_ref, g_ref, wa_ref, wb_ref, wc_ref, cw_ref, gco_ref, gqk_ref, ctx_ref,
                    yc_ref, q_ref, k_ref, v_ref, tail_ref,
                    hn_ref, cu_ref, carry_ref, *, tm, tiles_per_seq):
    i = pl.program_id(0)
    j = pl.program_id(1)
    cc = COL_CHUNK
    n_chunks = wa_ref.shape[1] // cc

    def proj(w_ref, c):
        return jnp.dot(hn_ref[...], w_ref[:, c * cc:(c + 1) * cc], preferred_element_type=F32)

    @pl.when(j == 0)
    def _conv():
        hn_ref[...] = _rms(x_ref[...], g_ref[...]).astype(BF16)

        @pl.when((i % tiles_per_seq) == 0)
        def _():
            carry_ref[...] = ctx_ref[...]

        for c in range(n_chunks):
            cs = slice(c * cc, (c + 1) * cc)
            gate_b = proj(wa_ref, c)
            cu = proj(wb_ref, c) * proj(wc_ref, c)
            cu_ref[c, SUBLANES:SUBLANES + tm, :] = cu
            cu_ref[c, 0:SUBLANES, :] = carry_ref[:, cs]
            tail = cu[tm - SUBLANES:tm, :]
            carry_ref[:, cs] = tail
            tail_ref[0, :, cs] = tail
            c1 = cu_ref[c, SUBLANES - 1:SUBLANES - 1 + tm, :]
            c2 = cu_ref[c, SUBLANES - 2:SUBLANES - 2 + tm, :]
            w = cw_ref[:, cs]
            y = gate_b * (w[0:1, :] * c2 + w[1:2, :] * c1 + w[2:3, :] * cu)
            for s in range(c * cc, (c + 1) * cc, HEAD_DIM):
                sl = slice(s, s + HEAD_DIM)
                yc_ref[:, sl] = _rms(y[:, s - c * cc:s - c * cc + HEAD_DIM],
                                     gco_ref[:, sl]).astype(BF16)

    @pl.when(j == 1)
    def _qkv():
        gqk = gqk_ref[...]
        for w_ref, o_ref, gain in ((wa_ref, q_ref, gqk[0:1, :]), (wb_ref, k_ref, gqk[1:2, :])):
            for c in range(n_chunks):
                acc = proj(w_ref, c)
                for s in range(0, cc, HEAD_DIM):
                    o_ref[:, c * cc + s:c * cc + s + HEAD_DIM] = _rms(
                        acc[:, s:s + HEAD_DIM], gain).astype(BF16)
        for c in range(n_chunks):
            v_ref[:, c * cc:(c + 1) * cc] = proj(wc_ref, c).astype(BF16)


def _in_proj(h, g_mix, w_all, layer, conv_w, g_co, g_qk, ctx_tail, *, tm, tiles_per_seq):
    m, d = h.shape
    width = conv_w.shape[1]
    assert w_all.shape[2] == 6 * width and width % COL_CHUNK == 0 and m % tm == 0

    def section(s):
        return pl.BlockSpec((None, d, width), lambda i, j: (layer, 0, 3 * j + s))

    kern = functools.partial(_in_proj_kernel, tm=tm, tiles_per_seq=tiles_per_seq)
    act = jax.ShapeDtypeStruct((m, width), BF16)
    row = pl.BlockSpec((tm, width), lambda i, j: (i, 0))
    const = lambda i, j: (0, 0)
    return pl.pallas_call(
        kern,
        grid=(m // tm, 2),
        in_specs=[
            pl.BlockSpec((tm, d), lambda i, j: (i, 0)),
            pl.BlockSpec((1, d), const),
            section(0), section(1), section(2),
            pl.BlockSpec((CONV_K, width), const),
            pl.BlockSpec((1, width), const),
            pl.BlockSpec((2, HEAD_DIM), const),
            pl.BlockSpec((SUBLANES, width), const),
        ],
        out_specs=[row, row, row, row,
                   pl.BlockSpec((1, SUBLANES, width), lambda i, j: (i, 0, 0))],
        out_shape=[act, act, act, act,
                   jax.ShapeDtypeStruct((m // tm, SUBLANES, width), F32)],
        scratch_shapes=[
            pltpu.VMEM((tm, d), BF16),
            pltpu.VMEM((width // COL_CHUNK, tm + SUBLANES, COL_CHUNK), F32),
            pltpu.VMEM((SUBLANES, width), F32),
        ],
        compiler_params=pltpu.CompilerParams(
            dimension_semantics=("arbitrary", "arbitrary"),
            vmem_limit_bytes=VMEM_LIMIT),
        name="in_proj",
    )(h, g_mix, w_all, w_all, w_all, conv_w, g_co, g_qk, ctx_tail)


def _neg_abs(x):
    bits = lax.bitcast_convert_type(x, jnp.int32) | jnp.int32(-2 ** 31)
    return lax.bitcast_convert_type(bits, F32)


def _neg_suffix_matrix(n):
    rr = lax.broadcasted_iota(jnp.int32, (2 * n, n), 0)
    cc = lax.broadcasted_iota(jnp.int32, (2 * n, n), 1)
    rr = jnp.where(rr >= n, rr - n, rr)
    return jnp.where(rr >= cc, -1.0, 0.0).astype(BF16)


def _attn_kernel(*refs, seq, tq, own_pad, ctx_pad):
    has_ctx = ctx_pad is not None
    if has_ctx:
        (q_ref, k_ref, v_ref, kc_ref, vc_ref, g_ref, o_ref,
         uu_ref, uc_ref, acc_ref, r_ref) = refs
    else:
        q_ref, k_ref, v_ref, g_ref, o_ref, uu_ref, acc_ref, r_ref = refs
    tk = tq
    nq = seq // tq
    heads = q_ref.shape[2] // HEAD_DIM
    nt = (((1,), (1,)), ((), ()))

    uu_ref[...] = _neg_suffix_matrix(tk)
    if has_ctx:
        uc_ref[...] = _neg_suffix_matrix(kc_ref.shape[1])
    rr = lax.broadcasted_iota(jnp.int32, (tq, tk), 0)
    cc = lax.broadcasted_iota(jnp.int32, (tq, tk), 1)
    diag_mask = cc < rr
    if own_pad:
        diag_mask = jnp.logical_and(diag_mask, cc >= own_pad)

    def hcols(hh):
        return slice(hh * HEAD_DIM, (hh + 1) * HEAD_DIM)

    def sweep(qs, blocks, uu):
        hs = range(heads)
        zs = [[lax.dot_general(qs[hh], kbs[hh], nt, preferred_element_type=F32) for hh in hs]
              for kbs, _, _ in blocks]
        css = []
        for zb, (_, _, mask) in zip(zs, blocks):
            row = []
            for z in zb:
                nl = jnp.maximum(z, 0.0) + jnp.log2(1.0 + jnp.exp2(_neg_abs(z)))
                if mask is not None:
                    nl = jnp.where(mask, nl, 0.0)
                hi = nl.astype(BF16)
                lo = (nl - hi.astype(F32)).astype(BF16)
                row.append(jnp.dot(jnp.concatenate([hi, lo], axis=1), uu,
                                   preferred_element_type=F32))
            css.append(row)
        rs = [r_ref[hh] for hh in hs]
        avs = [None] * heads
        for zb, cb, (kbs, vbs, mask) in zip(zs, css, blocks):
            width = kbs[0].shape[0]
            for hh in hs:
                t = zb[hh] + cb[hh] + jnp.concatenate([rs[hh]] * (width // HEAD_DIM), axis=1)
                if mask is not None:
                    t = jnp.where(mask, t, NEG_BIG)
                a = jnp.exp2(t).astype(BF16)
                av = jnp.dot(a, vbs[hh], preferred_element_type=F32)
                avs[hh] = av if avs[hh] is None else avs[hh] + av
                rs[hh] = rs[hh] + jnp.broadcast_to(cb[hh][:, 0:1], (tq, HEAD_DIM))
        for hh in hs:
            r_ref[hh] = rs[hh]
            acc_ref[hh] += avs[hh]

    def kv_block(c0, mask):
        return ([k_ref[0, pl.ds(c0, tk), hcols(hh)] for hh in range(heads)],
                [v_ref[0, pl.ds(c0, tk), hcols(hh)] for hh in range(heads)], mask)

    def alive():
        return jnp.max(r_ref[...]) > DEAD_LOG2

    def q_block(qi):
        first = isinstance(qi, int) and qi == 0
        row0 = qi * tq if isinstance(qi, int) else pl.multiple_of(qi * tq, tq)
        qs = [q_ref[0, pl.ds(row0, tq), hcols(hh)] for hh in range(heads)]
        acc_ref[...] = jnp.zeros_like(acc_ref)
        r_ref[...] = jnp.zeros_like(r_ref)
        uu = uu_ref[...]

        if first:
            sweep(qs, [kv_block(row0, diag_mask)], uu)
            live = alive()
        else:
            sweep(qs, [kv_block(row0, diag_mask),
                       kv_block(pl.multiple_of(row0 - tk, tk), None)], uu)

            def kv_cond(carry):
                return jnp.logical_and(carry[0] < qi, carry[1])

            def kv_step(carry):
                n = carry[0]
                sweep(qs, [kv_block(pl.multiple_of((qi - 1 - n) * tk, tk), None)], uu)
                return n + 1, alive()

            _, live = lax.while_loop(kv_cond, kv_step, (jnp.int32(1), alive()))

        if has_ctx:
            @pl.when(live)
            def _():
                cw = kc_ref.shape[1]
                ctx_mask = lax.broadcasted_iota(jnp.int32, (tq, cw), 1) >= ctx_pad
                sweep(qs, [([kc_ref[0, :, hcols(hh)] for hh in range(heads)],
                            [vc_ref[0, :, hcols(hh)] for hh in range(heads)], ctx_mask)],
                      uc_ref[...])

        g = g_ref[...]
        for hh in range(heads):
            o_ref[0, pl.ds(row0, tq), hcols(hh)] = _rms(acc_ref[hh], g[:, hcols(hh)]).astype(BF16)

    q_block(0)
    if nq > 1:
        def q_step(qi, carry):
            q_block(qi)
            return carry
        lax.fori_loop(1, nq, q_step, 0)


def _attention(q, k, v, ctx, g_ao, *, tq, heads, own_pad, ctx_pad):
    b, seq, width = q.shape
    hw = heads * HEAD_DIM
    blk = pl.BlockSpec((1, seq, hw), lambda bi, hi: (bi, 0, hi))
    in_specs = [blk, blk, blk]
    args = [q, k, v]
    scratch = [pltpu.VMEM((2 * tq, tq), BF16)]
    if ctx is not None:
        kc, vc = ctx
        cblk = pl.BlockSpec((1, kc.shape[1], hw), lambda bi, hi: (0, 0, hi))
        in_specs += [cblk, cblk]
        args += [kc, vc]
        scratch.append(pltpu.VMEM((2 * kc.shape[1], kc.shape[1]), BF16))
    in_specs.append(pl.BlockSpec((1, hw), lambda bi, hi: (0, hi)))
    args.append(g_ao)
    kern = functools.partial(_attn_kernel, seq=seq, tq=tq, own_pad=own_pad,
                             ctx_pad=ctx_pad if ctx is not None else None)
    return pl.pallas_call(
        kern,
        grid=(b, width // hw),
        in_specs=in_specs,
        out_specs=blk,
        out_shape=jax.ShapeDtypeStruct((b, seq, width), BF16),
        scratch_shapes=scratch + [
            pltpu.VMEM((heads, tq, HEAD_DIM), F32),
            pltpu.VMEM((heads, tq, HEAD_DIM), F32),
        ],
        compiler_params=pltpu.CompilerParams(
            dimension_semantics=("arbitrary", "arbitrary"),
            vmem_limit_bytes=VMEM_LIMIT),
        name="sb_attention",
    )(*args)


def _out_proj_kernel(h_ref, yc_ref, ya_ref, w_ref, o_ref):
    half = yc_ref.shape[1]
    o_ref[...] = (h_ref[...]
                  + jnp.dot(yc_ref[...], w_ref[0:half, :], preferred_element_type=F32)
                  + jnp.dot(ya_ref[...], w_ref[half:, :], preferred_element_type=F32))


def _out_proj(h, yc, ya, w_out, layer, *, tm):
    m, d = h.shape
    half = yc.shape[1]
    row = pl.BlockSpec((tm, d), lambda i: (i, 0))
    yrow = pl.BlockSpec((tm, half), lambda i: (i, 0))
    return pl.pallas_call(
        _out_proj_kernel,
        grid=(m // tm,),
        in_specs=[row, yrow, yrow,
                  pl.BlockSpec((None,) + w_out.shape[1:], lambda i: (layer, 0, 0))],
        out_specs=row,
        out_shape=jax.ShapeDtypeStruct((m, d), F32),
        compiler_params=pltpu.CompilerParams(
            dimension_semantics=("arbitrary",), vmem_limit_bytes=VMEM_LIMIT),
        name="out_proj",
    )(h, yc, ya, w_out)


def _mlp_kernel(h_ref, g_ref, w1_ref, w2_ref, o_ref, hn_ref):
    f = pl.program_id(1)

    @pl.when(f == 0)
    def _():
        x = h_ref[...]
        hn_ref[...] = _rms(x, g_ref[...]).astype(BF16)
        o_ref[...] = x

    hid = jnp.dot(hn_ref[...], w1_ref[...], preferred_element_type=F32)
    hid = jnp.square(jnp.maximum(hid, 0.0)).astype(BF16)
    o_ref[...] += jnp.dot(hid, w2_ref[...], preferred_element_type=F32)


def _mlp(h, g_mlp, w1, w2, layer, *, tm, tf):
    m, d = h.shape
    d_ff = w1.shape[2]
    row = pl.BlockSpec((tm, d), lambda i, f: (i, 0))
    return pl.pallas_call(
        _mlp_kernel,
        grid=(m // tm, d_ff // tf),
        in_specs=[row,
                  pl.BlockSpec((1, d), lambda i, f: (0, 0)),
                  pl.BlockSpec((None, d, tf), lambda i, f: (layer, 0, f)),
                  pl.BlockSpec((None, tf, d), lambda i, f: (layer, f, 0))],
        out_specs=row,
        out_shape=jax.ShapeDtypeStruct((m, d), F32),
        scratch_shapes=[pltpu.VMEM((tm, d), BF16)],
        compiler_params=pltpu.CompilerParams(
            dimension_semantics=("arbitrary", "arbitrary"), vmem_limit_bytes=VMEM_LIMIT),
        name="mlp",
    )(h, g_mlp, w1, w2)


def _tiles(seq):
    return dict(tm_in=min(512, seq), tm_out=min(512, seq), tm_mlp=min(1024, seq), tf=512,
                tq=min(256, seq), heads=4)


def kernel(x, meta_tokens, g_mix, w_in, conv_w, g_q, g_k, g_conv_out, g_attn_out, w_out,
           g_mlp, w_mlp_in, w_mlp_out):
    b, seq, d = x.shape
    depth = w_in.shape[0]
    n_meta = meta_tokens.shape[0]
    width = conv_w.shape[2]
    assert n_meta <= META_ROWS and seq % META_ROWS == 0
    cfg = _tiles(seq)
    meta_pad = META_ROWS - n_meta

    h = x.reshape(b * seq, d)
    hm = jnp.pad(meta_tokens.astype(x.dtype), ((meta_pad, 0), (0, 0)))
    zero_tail = jnp.zeros((SUBLANES, width), F32)
    w_in_b = w_in.astype(BF16)
    w_out_b = w_out.astype(BF16)
    w1_b = w_mlp_in.astype(BF16)
    w2_b = w_mlp_out.astype(BF16)

    for i in range(depth):
        g_mix_i = g_mix[i].reshape(1, d)
        g_co_i = g_conv_out[i].reshape(1, width)
        g_ao_i = g_attn_out[i].reshape(1, width)
        g_mlp_i = g_mlp[i].reshape(1, d)
        g_qk_i = jnp.stack([g_q[i] * (HEAD_DIM ** -0.5 * LOG2E), g_k[i]])

        ycm, qm, km, vm, tail_m = _in_proj(hm, g_mix_i, w_in_b, i, conv_w[i], g_co_i, g_qk_i,
                                           zero_tail, tm=META_ROWS, tiles_per_seq=1)
        yc, q, k, v, _ = _in_proj(h, g_mix_i, w_in_b, i, conv_w[i], g_co_i, g_qk_i, tail_m[0],
                                  tm=cfg["tm_in"], tiles_per_seq=seq // cfg["tm_in"])
        ya = _attention(q.reshape(b, seq, width), k.reshape(b, seq, width),
                        v.reshape(b, seq, width), (km[None], vm[None]), g_ao_i,
                        tq=cfg["tq"], heads=cfg["heads"], own_pad=0, ctx_pad=meta_pad)
        h = _out_proj(h, yc, ya.reshape(b * seq, width), w_out_b, i, tm=cfg["tm_out"])
        h = _mlp(h, g_mlp_i, w1_b, w2_b, i, tm=cfg["tm_mlp"], tf=cfg["tf"])

        if i + 1 < depth:
            yam = _attention(qm[None], km[None], vm[None], None, g_ao_i,
                             tq=META_ROWS, heads=cfg["heads"], own_pad=meta_pad, ctx_pad=None)
            hm = _out_proj(hm, ycm, yam[0], w_out_b, i, tm=META_ROWS)
            hm = _mlp(hm, g_mlp_i, w1_b, w2_b, i, tm=META_ROWS, tf=cfg["tf"])

    return h.reshape(b, seq, d)
```

```python
import functools

import jax
import jax.numpy as jnp
from jax import lax
from jax.experimental import pallas as pl
from jax.experimental.pallas import tpu as pltpu

F32 = jnp.float32
BF16 = jnp.bfloat16

EPS = 1e-6
HEAD_DIM = 128
CONV_K = 3
SUBLANES = 8
META_ROWS = 128
COL_CHUNK = 2 * HEAD_DIM
VMEM_LIMIT = 56 * 1024 * 1024
NEG_BIG = -1e30
LOG2E = 1.4426950408889634
DEAD_LOG2 = -160.0


def _rms(x, g):
    ms = jnp.mean(x * x, axis=-1, keepdims=True)
    return x * lax.rsqrt(ms + EPS) * g


def _in_proj_kernel(x_ref, g_ref, w_ref, cw_ref, gco_ref, gqk_ref, ctx_ref,
                    yc_ref, q_ref, k_ref, v_ref, tail_ref,
                    hn_ref, cu_ref, carry_ref, *, tm, tiles_per_seq):
    step = pl.program_id(0)
    cc = COL_CHUNK
    width = yc_ref.shape[1]
    n_chunks = width // cc

    @pl.when(step == 0)
    def _():
        hn_ref[0] = _rms(x_ref[...], g_ref[...]).astype(BF16)

    @pl.when(jnp.logical_and(step > 0, ((step - 1) % tiles_per_seq) == 0))
    def _():
        carry_ref[...] = ctx_ref[...]

    @pl.when(step > 0)
    def _tile():
        slot = (step - 1) % 2

        def proj(section, c):
            col = section * width + c * cc
            return jnp.dot(hn_ref[slot], w_ref[:, col:col + cc], preferred_element_type=F32)

        for c in range(n_chunks):
            cs = slice(c * cc, (c + 1) * cc)
            gate_b = proj(0, c)
            cu = proj(1, c) * proj(2, c)
            if c == 0:
                hn_ref[1 - slot] = _rms(x_ref[...], g_ref[...]).astype(BF16)
            cu_ref[c, SUBLANES:SUBLANES + tm, :] = cu
            cu_ref[c, 0:SUBLANES, :] = carry_ref[:, cs]
            tail = cu[tm - SUBLANES:tm, :]
            carry_ref[:, cs] = tail
            tail_ref[0, :, cs] = tail
            c1 = cu_ref[c, SUBLANES - 1:SUBLANES - 1 + tm, :]
            c2 = cu_ref[c, SUBLANES - 2:SUBLANES - 2 + tm, :]
            w = cw_ref[:, cs]
            y = gate_b * (w[0:1, :] * c2 + w[1:2, :] * c1 + w[2:3, :] * cu)
            for s in range(c * cc, (c + 1) * cc, HEAD_DIM):
                sl = slice(s, s + HEAD_DIM)
                yc_ref[:, sl] = _rms(y[:, s - c * cc:s - c * cc + HEAD_DIM],
                                     gco_ref[:, sl]).astype(BF16)

        gqk = gqk_ref[...]
        for section, o_ref, gain in ((3, q_ref, gqk[0:1, :]), (4, k_ref, gqk[1:2, :])):
            for c in range(n_chunks):
                acc = proj(section, c)
                for s in range(0, cc, HEAD_DIM):
                    o_ref[:, c * cc + s:c * cc + s + HEAD_DIM] = _rms(
                        acc[:, s:s + HEAD_DIM], gain).astype(BF16)
        for c in range(n_chunks):
            v_ref[:, c * cc:(c + 1) * cc] = proj(5, c).astype(BF16)


def _in_proj(h, g_mix, w_all, layer, conv_w, g_co, g_qk, ctx_tail, *, tm, tiles_per_seq):
    m, d = h.shape
    width = conv_w.shape[1]
    n_tiles = m // tm
    assert w_all.shape[2] == 6 * width and width % COL_CHUNK == 0 and m % tm == 0

    kern = functools.partial(_in_proj_kernel, tm=tm, tiles_per_seq=tiles_per_seq)
    act = jax.ShapeDtypeStruct((m, width), BF16)
    row = pl.BlockSpec((tm, width), lambda s: (jnp.maximum(s - 1, 0), 0))
    const = lambda s: (0, 0)
    return pl.pallas_call(
        kern,
        grid=(n_tiles + 1,),
        in_specs=[
            pl.BlockSpec((tm, d), lambda s: (jnp.minimum(s, n_tiles - 1), 0)),
            pl.BlockSpec((1, d), const),
            pl.BlockSpec((None, d, 6 * width), lambda s: (layer, 0, 0),
                         pipeline_mode=pl.Buffered(1)),
            pl.BlockSpec((CONV_K, width), const),
            pl.BlockSpec((1, width), const),
            pl.BlockSpec((2, HEAD_DIM), const),
            pl.BlockSpec((SUBLANES, width), const),
        ],
        out_specs=[row, row, row, row,
                   pl.BlockSpec((1, SUBLANES, width), lambda s: (jnp.maximum(s - 1, 0), 0, 0))],
        out_shape=[act, act, act, act,
                   jax.ShapeDtypeStruct((n_tiles, SUBLANES, width), F32)],
        scratch_shapes=[
            pltpu.VMEM((2, tm, d), BF16),
            pltpu.VMEM((width // COL_CHUNK, tm + SUBLANES, COL_CHUNK), F32),
            pltpu.VMEM((SUBLANES, width), F32),
        ],
        compiler_params=pltpu.CompilerParams(
            dimension_semantics=("arbitrary",),
            vmem_limit_bytes=VMEM_LIMIT),
        name="in_proj",
    )(h, g_mix, w_all, conv_w, g_co, g_qk, ctx_tail)


def _neg_abs(x):
    bits = lax.bitcast_convert_type(x, jnp.int32) | jnp.int32(-2 ** 31)
    return lax.bitcast_convert_type(bits, F32)


def _neg_suffix_matrix(n):
    rr = lax.broadcasted_iota(jnp.int32, (2 * n, n), 0)
    cc = lax.broadcasted_iota(jnp.int32, (2 * n, n), 1)
    rr = jnp.where(rr >= n, rr - n, rr)
    return jnp.where(rr >= cc, -1.0, 0.0).astype(BF16)


def _attn_kernel(*refs, seq, tq, own_pad, ctx_pad):
    has_ctx = ctx_pad is not None
    if has_ctx:
        (q_ref, k_ref, v_ref, kc_ref, vc_ref, g_ref, o_ref,
         uu_ref, uc_ref, acc_ref, r_ref) = refs
    else:
        q_ref, k_ref, v_ref, g_ref, o_ref, uu_ref, acc_ref, r_ref = refs
    tk = tq
    nq = seq // tq
    heads = q_ref.shape[2] // HEAD_DIM
    nt = (((1,), (1,)), ((), ()))

    uu_ref[...] = _neg_suffix_matrix(tk)
    if has_ctx:
        uc_ref[...] = _neg_suffix_matrix(kc_ref.shape[1])
    rr = lax.broadcasted_iota(jnp.int32, (tq, tk), 0)
    cc = lax.broadcasted_iota(jnp.int32, (tq, tk), 1)
    diag_mask = cc < rr
    if own_pad:
        diag_mask = jnp.logical_and(diag_mask, cc >= own_pad)

    def hcols(hh):
        return slice(hh * HEAD_DIM, (hh + 1) * HEAD_DIM)

    def sweep(qs, blocks, uu):
        hs = range(heads)
        zs = [[lax.dot_general(qs[hh], kbs[hh], nt, preferred_element_type=F32) for hh in hs]
              for kbs, _, _ in blocks]
        css = []
        for zb, (_, _, mask) in zip(zs, blocks):
            row = []
            for z in zb:
                nl = jnp.maximum(z, 0.0) + jnp.log2(1.0 + jnp.exp2(_neg_abs(z)))
                if mask is not None:
                    nl = jnp.where(mask, nl, 0.0)
                hi = nl.astype(BF16)
                lo = (nl - hi.astype(F32)).astype(BF16)
                row.append(jnp.dot(jnp.concatenate([hi, lo], axis=1), uu,
                                   preferred_element_type=F32))
            css.append(row)
        rs = [r_ref[hh] for hh in hs]
        avs = [None] * heads
        for zb, cb, (kbs, vbs, mask) in zip(zs, css, blocks):
            width = kbs[0].shape[0]
            for hh in hs:
                t = zb[hh] + cb[hh] + jnp.concatenate([rs[hh]] * (width // HEAD_DIM), axis=1)
                if mask is not None:
                    t = jnp.where(mask, t, NEG_BIG)
                a = jnp.exp2(t).astype(BF16)
                av = jnp.dot(a, vbs[hh], preferred_element_type=F32)
                avs[hh] = av if avs[hh] is None else avs[hh] + av
                rs[hh] = rs[hh] + jnp.broadcast_to(cb[hh][:, 0:1], (tq, HEAD_DIM))
        for hh in hs:
            r_ref[hh] = rs[hh]
            acc_ref[hh] += avs[hh]

    def kv_block(c0, mask):
        return ([k_ref[0, pl.ds(c0, tk), hcols(hh)] for hh in range(heads)],
                [v_ref[0, pl.ds(c0, tk), hcols(hh)] for hh in range(heads)], mask)

    def alive():
        return jnp.max(r_ref[...]) > DEAD_LOG2

    def q_block(qi):
        first = isinstance(qi, int) and qi == 0
        row0 = qi * tq if isinstance(qi, int) else pl.multiple_of(qi * tq, tq)
        qs = [q_ref[0, pl.ds(row0, tq), hcols(hh)] for hh in range(heads)]
        acc_ref[...] = jnp.zeros_like(acc_ref)
        r_ref[...] = jnp.zeros_like(r_ref)
        uu = uu_ref[...]

        if first:
            sweep(qs, [kv_block(row0, diag_mask)], uu)
            live = alive()
        else:
            sweep(qs, [kv_block(row0, diag_mask),
                       kv_block(pl.multiple_of(row0 - tk, tk), None)], uu)

            def kv_cond(carry):
                return jnp.logical_and(carry[0] < qi, carry[1])

            def kv_step(carry):
                n = carry[0]
                sweep(qs, [kv_block(pl.multiple_of((qi - 1 - n) * tk, tk), None)], uu)
                return n + 1, alive()

            _, live = lax.while_loop(kv_cond, kv_step, (jnp.int32(1), alive()))

        if has_ctx:
            @pl.when(live)
            def _():
                cw = kc_ref.shape[1]
                ctx_mask = lax.broadcasted_iota(jnp.int32, (tq, cw), 1) >= ctx_pad
                sweep(qs, [([kc_ref[0, :, hcols(hh)] for hh in range(heads)],
                            [vc_ref[0, :, hcols(hh)] for hh in range(heads)], ctx_mask)],
                      uc_ref[...])

        g = g_ref[...]
        for hh in range(heads):
            o_ref[0, pl.ds(row0, tq), hcols(hh)] = _rms(acc_ref[hh], g[:, hcols(hh)]).astype(BF16)

    q_block(0)
    if nq > 1:
        def q_step(qi, carry):
            q_block(qi)
            return carry
        lax.fori_loop(1, nq, q_step, 0)


def _attention(q, k, v, ctx, g_ao, *, tq, heads, own_pad, ctx_pad):
    b, seq, width = q.shape
    hw = heads * HEAD_DIM
    blk = pl.BlockSpec((1, seq, hw), lambda bi, hi: (bi, 0, hi))
    in_specs = [blk, blk, blk]
    args = [q, k, v]
    scratch = [pltpu.VMEM((2 * tq, tq), BF16)]
    if ctx is not None:
        kc, vc = ctx
        cblk = pl.BlockSpec((1, kc.shape[1], hw), lambda bi, hi: (0, 0, hi))
        in_specs += [cblk, cblk]
        args += [kc, vc]
        scratch.append(pltpu.VMEM((2 * kc.shape[1], kc.shape[1]), BF16))
    in_specs.append(pl.BlockSpec((1, hw), lambda bi, hi: (0, hi)))
    args.append(g_ao)
    kern = functools.partial(_attn_kernel, seq=seq, tq=tq, own_pad=own_pad,
                             ctx_pad=ctx_pad if ctx is not None else None)
    return pl.pallas_call(
        kern,
        grid=(b, width // hw),
        in_specs=in_specs,
        out_specs=blk,
        out_shape=jax.ShapeDtypeStruct((b, seq, width), BF16),
        scratch_shapes=scratch + [
            pltpu.VMEM((heads, tq, HEAD_DIM), F32),
            pltpu.VMEM((heads, tq, HEAD_DIM), F32),
        ],
        compiler_params=pltpu.CompilerParams(
            dimension_semantics=("arbitrary", "arbitrary"),
            vmem_limit_bytes=VMEM_LIMIT),
        name="sb_attention",
    )(*args)


def _out_proj_kernel(h_ref, yc_ref, ya_ref, w_ref, o_ref):
    half = yc_ref.shape[1]
    o_ref[...] = (h_ref[...]
                  + jnp.dot(yc_ref[...], w_ref[0:half, :], preferred_element_type=F32)
                  + jnp.dot(ya_ref[...], w_ref[half:, :], preferred_element_type=F32))


def _out_proj(h, yc, ya, w_out, layer, *, tm):
    m, d = h.shape
    half = yc.shape[1]
    row = pl.BlockSpec((tm, d), lambda i: (i, 0))
    yrow = pl.BlockSpec((tm, half), lambda i: (i, 0))
    return pl.pallas_call(
        _out_proj_kernel,
        grid=(m // tm,),
        in_specs=[row, yrow, yrow,
                  pl.BlockSpec((None,) + w_out.shape[1:], lambda i: (layer, 0, 0))],
        out_specs=row,
        out_shape=jax.ShapeDtypeStruct((m, d), F32),
        compiler_params=pltpu.CompilerParams(
            dimension_semantics=("arbitrary",), vmem_limit_bytes=VMEM_LIMIT),
        name="out_proj",
    )(h, yc, ya, w_out)


def _mlp_kernel(h_ref, g_ref, w1_ref, w2_ref, o_ref, hn_ref):
    f = pl.program_id(1)

    @pl.when(f == 0)
    def _():
        x = h_ref[...]
        hn_ref[...] = _rms(x, g_ref[...]).astype(BF16)
        o_ref[...] = x

    hid = jnp.dot(hn_ref[...], w1_ref[...], preferred_element_type=F32)
    hid = jnp.square(jnp.maximum(hid, 0.0)).astype(BF16)
    o_ref[...] += jnp.dot(hid, w2_ref[...], preferred_element_type=F32)


def _mlp(h, g_mlp, w1, w2, layer, *, tm, tf):
    m, d = h.shape
    d_ff = w1.shape[2]
    row = pl.BlockSpec((tm, d), lambda i, f: (i, 0))
    return pl.pallas_call(
        _mlp_kernel,
        grid=(m // tm, d_ff // tf),
        in_specs=[row,
                  pl.BlockSpec((1, d), lambda i, f: (0, 0)),
                  pl.BlockSpec((None, d, tf), lambda i, f: (layer, 0, f)),
                  pl.BlockSpec((None, tf, d), lambda i, f: (layer, f, 0))],
        out_specs=row,
        out_shape=jax.ShapeDtypeStruct((m, d), F32),
        scratch_shapes=[pltpu.VMEM((tm, d), BF16)],
        compiler_params=pltpu.CompilerParams(
            dimension_semantics=("arbitrary", "arbitrary"), vmem_limit_bytes=VMEM_LIMIT),
        name="mlp",
    )(h, g_mlp, w1, w2)


def _tiles(seq):
    return dict(tm_in=min(512, seq), tm_out=min(512, seq), tm_mlp=min(1024, seq), tf=512,
                tq=min(256, seq), heads=4)


def kernel(x, meta_tokens, g_mix, w_in, conv_w, g_q, g_k, g_conv_out, g_attn_out, w_out,
           g_mlp, w_mlp_in, w_mlp_out):
    b, seq, d = x.shape
    depth = w_in.shape[0]
    n_meta = meta_tokens.shape[0]
    width = conv_w.shape[2]
    assert n_meta <= META_ROWS and seq % META_ROWS == 0
    cfg = _tiles(seq)
    meta_pad = META_ROWS - n_meta

    h = x.reshape(b * seq, d)
    hm = jnp.pad(meta_tokens.astype(x.dtype), ((meta_pad, 0), (0, 0)))
    zero_tail = jnp.zeros((SUBLANES, width), F32)
    w_in_b = w_in.astype(BF16)
    w_out_b = w_out.astype(BF16)
    w1_b = w_mlp_in.astype(BF16)
    w2_b = w_mlp_out.astype(BF16)

    for i in range(depth):
        g_mix_i = g_mix[i].reshape(1, d)
        g_co_i = g_conv_out[i].reshape(1, width)
        g_ao_i = g_attn_out[i].reshape(1, width)
        g_mlp_i = g_mlp[i].reshape(1, d)
        g_qk_i = jnp.stack([g_q[i] * (HEAD_DIM ** -0.5 * LOG2E), g_k[i]])

        ycm, qm, km, vm, tail_m = _in_proj(hm, g_mix_i, w_in_b, i, conv_w[i], g_co_i, g_qk_i,
                                           zero_tail, tm=META_ROWS, tiles_per_seq=1)
        yc, q, k, v, _ = _in_proj(h, g_mix_i, w_in_b, i, conv_w[i], g_co_i, g_qk_i, tail_m[0],
                                  tm=cfg["tm_in"], tiles_per_seq=seq // cfg["tm_in"])
        ya = _attention(q.reshape(b, seq, width), k.reshape(b, seq, width),
                        v.reshape(b, seq, width), (km[None], vm[None]), g_ao_i,
                        tq=cfg["tq"], heads=cfg["heads"], own_pad=0, ctx_pad=meta_pad)
        h = _out_proj(h, yc, ya.reshape(b * seq, width), w_out_b, i, tm=cfg["tm_out"])
        h = _mlp(h, g_mlp_i, w1_b, w2_b, i, tm=cfg["tm_mlp"], tf=cfg["tf"])

        if i + 1 < depth:
            yam = _attention(qm[None], km[None], vm[None], None, g_ao_i,
                             tq=META_ROWS, heads=cfg["heads"], own_pad=meta_pad, ctx_pad=None)
            hm = _out_proj(hm, ycm, yam[0], w_out_b, i, tm=META_ROWS)
            hm = _mlp(hm, g_mlp_i, w1_b, w2_b, i, tm=META_ROWS, tf=cfg["tf"])

    return h.reshape(b, seq, d)
```

```python
import functools

import jax
import jax.numpy as jnp
from jax import lax
from jax.experimental import pallas as pl
from jax.experimental.pallas import tpu as pltpu

F32 = jnp.float32
BF16 = jnp.bfloat16

EPS = 1e-6
HEAD_DIM = 128
CONV_K = 3
SUBLANES = 8
META_ROWS = 128
COL_CHUNK = 2 * HEAD_DIM
VMEM_LIMIT = 60 * 1024 * 1024
NEG_BIG = -1e30
LOG2E = 1.4426950408889634
DEAD_LOG2 = -160.0


def _rms(x, g):
    ms = jnp.mean(x * x, axis=-1, keepdims=True)
    return x * lax.rsqrt(ms + EPS) * g


def _in_proj_kernel(x_ref, g_ref, w_ref, cw_ref, gco_ref, gqk_ref, ctx_ref,
                    yc_ref, q_ref, k_ref, v_ref, tail_ref,
                    hn_ref, cu_ref, carry_ref, *, tm, tiles_per_seq):
    step = pl.program_id(0)
    cc = COL_CHUNK
    width = yc_ref.shape[1]
    n_chunks = width // cc

    @pl.when(step == 0)
    def _():
        hn_ref[0] = _rms(x_ref[...], g_ref[...]).astype(BF16)

    @pl.when(jnp.logical_and(step > 0, ((step - 1) % tiles_per_seq) == 0))
    def _():
        carry_ref[...] = ctx_ref[...]

    @pl.when(step > 0)
    def _tile():
        slot = (step - 1) % 2

        def proj(section, c):
            col = section * width + c * cc
            return jnp.dot(hn_ref[slot], w_ref[:, col:col + cc], preferred_element_type=F32)

        for c in range(n_chunks):
            cs = slice(c * cc, (c + 1) * cc)
            gate_b = proj(0, c)
            cu = proj(1, c) * proj(2, c)
            if c == 0:
                hn_ref[1 - slot] = _rms(x_ref[...], g_ref[...]).astype(BF16)
            cu_ref[c, SUBLANES:SUBLANES + tm, :] = cu
            cu_ref[c, 0:SUBLANES, :] = carry_ref[:, cs]
            tail = cu[tm - SUBLANES:tm, :]
            carry_ref[:, cs] = tail
            tail_ref[0, :, cs] = tail
            c1 = cu_ref[c, SUBLANES - 1:SUBLANES - 1 + tm, :]
            c2 = cu_ref[c, SUBLANES - 2:SUBLANES - 2 + tm, :]
            w = cw_ref[:, cs]
            y = gate_b * (w[0:1, :] * c2 + w[1:2, :] * c1 + w[2:3, :] * cu)
            for s in range(c * cc, (c + 1) * cc, HEAD_DIM):
                sl = slice(s, s + HEAD_DIM)
                yc_ref[:, sl] = _rms(y[:, s - c * cc:s - c * cc + HEAD_DIM],
                                     gco_ref[:, sl]).astype(BF16)

        gqk = gqk_ref[...]
        for section, o_ref, gain in ((3, q_ref, gqk[0:1, :]), (4, k_ref, gqk[1:2, :])):
            for c in range(n_chunks):
                acc = proj(section, c)
                for s in range(0, cc, HEAD_DIM):
                    o_ref[:, c * cc + s:c * cc + s + HEAD_DIM] = _rms(
                        acc[:, s:s + HEAD_DIM], gain).astype(BF16)
        for c in range(n_chunks):
            v_ref[:, c * cc:(c + 1) * cc] = proj(5, c).astype(BF16)


def _in_proj(h, g_mix, w_all, layer, conv_w, g_co, g_qk, ctx_tail, *, tm, tiles_per_seq):
    m, d = h.shape
    width = conv_w.shape[1]
    n_tiles = m // tm
    assert w_all.shape[2] == 6 * width and width % COL_CHUNK == 0 and m % tm == 0

    kern = functools.partial(_in_proj_kernel, tm=tm, tiles_per_seq=tiles_per_seq)
    act = jax.ShapeDtypeStruct((m, width), BF16)
    row = pl.BlockSpec((tm, width), lambda s: (jnp.maximum(s - 1, 0), 0))
    const = lambda s: (0, 0)
    return pl.pallas_call(
        kern,
        grid=(n_tiles + 1,),
        in_specs=[
            pl.BlockSpec((tm, d), lambda s: (jnp.minimum(s, n_tiles - 1), 0)),
            pl.BlockSpec((1, d), const),
            pl.BlockSpec((None, d, 6 * width), lambda s: (layer, 0, 0),
                         pipeline_mode=pl.Buffered(1)),
            pl.BlockSpec((CONV_K, width), const),
            pl.BlockSpec((1, width), const),
            pl.BlockSpec((2, HEAD_DIM), const),
            pl.BlockSpec((SUBLANES, width), const),
        ],
        out_specs=[row, row, row, row,
                   pl.BlockSpec((1, SUBLANES, width), lambda s: (jnp.maximum(s - 1, 0), 0, 0))],
        out_shape=[act, act, act, act,
                   jax.ShapeDtypeStruct((n_tiles, SUBLANES, width), F32)],
        scratch_shapes=[
            pltpu.VMEM((2, tm, d), BF16),
            pltpu.VMEM((width // COL_CHUNK, tm + SUBLANES, COL_CHUNK), F32),
            pltpu.VMEM((SUBLANES, width), F32),
        ],
        compiler_params=pltpu.CompilerParams(
            dimension_semantics=("arbitrary",),
            vmem_limit_bytes=VMEM_LIMIT),
        name="in_proj",
    )(h, g_mix, w_all, conv_w, g_co, g_qk, ctx_tail)


def _neg_abs(x):
    bits = lax.bitcast_convert_type(x, jnp.int32) | jnp.int32(-2 ** 31)
    return lax.bitcast_convert_type(bits, F32)


def _neg_suffix_matrix(n):
    rr = lax.broadcasted_iota(jnp.int32, (2 * n, n), 0)
    cc = lax.broadcasted_iota(jnp.int32, (2 * n, n), 1)
    rr = jnp.where(rr >= n, rr - n, rr)
    return jnp.where(rr >= cc, -1.0, 0.0).astype(BF16)


def _attn_kernel(*refs, seq, tq, own_pad, ctx_pad):
    has_ctx = ctx_pad is not None
    if has_ctx:
        (q_ref, k_ref, v_ref, kc_ref, vc_ref, g_ref, o_ref,
         uu_ref, uc_ref, acc_ref, r_ref) = refs
    else:
        q_ref, k_ref, v_ref, g_ref, o_ref, uu_ref, acc_ref, r_ref = refs
    tk = tq
    nq = seq // tq
    heads = q_ref.shape[2] // HEAD_DIM
    nt = (((1,), (1,)), ((), ()))

    uu_ref[...] = _neg_suffix_matrix(tk)
    if has_ctx:
        uc_ref[...] = _neg_suffix_matrix(kc_ref.shape[1])
    rr = lax.broadcasted_iota(jnp.int32, (tq, tk), 0)
    cc = lax.broadcasted_iota(jnp.int32, (tq, tk), 1)
    diag_mask = cc < rr
    if own_pad:
        diag_mask = jnp.logical_and(diag_mask, cc >= own_pad)

    def hcols(hh):
        return slice(hh * HEAD_DIM, (hh + 1) * HEAD_DIM)

    def sweep(qs, blocks, uu):
        hs = range(heads)
        zs = [[lax.dot_general(qs[hh], kbs[hh], nt, preferred_element_type=F32) for hh in hs]
              for kbs, _, _ in blocks]
        css = []
        for zb, (_, _, mask) in zip(zs, blocks):
            row = []
            for z in zb:
                nl = jnp.maximum(z, 0.0) + jnp.log2(1.0 + jnp.exp2(_neg_abs(z)))
                if mask is not None:
                    nl = jnp.where(mask, nl, 0.0)
                hi = nl.astype(BF16)
                lo = (nl - hi.astype(F32)).astype(BF16)
                row.append(jnp.dot(jnp.concatenate([hi, lo], axis=1), uu,
                                   preferred_element_type=F32))
            css.append(row)
        rs = [r_ref[hh] for hh in hs]
        avs = [None] * heads
        for zb, cb, (kbs, vbs, mask) in zip(zs, css, blocks):
            width = kbs[0].shape[0]
            for hh in hs:
                t = zb[hh] + cb[hh] + jnp.concatenate([rs[hh]] * (width // HEAD_DIM), axis=1)
                if mask is not None:
                    t = jnp.where(mask, t, NEG_BIG)
                a = jnp.exp2(t).astype(BF16)
                av = jnp.dot(a, vbs[hh], preferred_element_type=F32)
                avs[hh] = av if avs[hh] is None else avs[hh] + av
                rs[hh] = rs[hh] + jnp.broadcast_to(cb[hh][:, 0:1], (tq, HEAD_DIM))
        for hh in hs:
            r_ref[hh] = rs[hh]
            acc_ref[hh] += avs[hh]

    def kv_block(c0, mask):
        return ([k_ref[0, pl.ds(c0, tk), hcols(hh)] for hh in range(heads)],
                [v_ref[0, pl.ds(c0, tk), hcols(hh)] for hh in range(heads)], mask)

    def alive():
        return jnp.max(r_ref[...]) > DEAD_LOG2

    def q_block(qi):
        first = isinstance(qi, int) and qi == 0
        row0 = qi * tq if isinstance(qi, int) else pl.multiple_of(qi * tq, tq)
        qs = [q_ref[0, pl.ds(row0, tq), hcols(hh)] for hh in range(heads)]
        acc_ref[...] = jnp.zeros_like(acc_ref)
        r_ref[...] = jnp.zeros_like(r_ref)
        uu = uu_ref[...]

        if first:
            sweep(qs, [kv_block(row0, diag_mask)], uu)
            live = alive()
        else:
            sweep(qs, [kv_block(row0, diag_mask),
                       kv_block(pl.multiple_of(row0 - tk, tk), None)], uu)

            def kv_cond(carry):
                return jnp.logical_and(carry[0] < qi, carry[1])

            def kv_step(carry):
                n = carry[0]
                sweep(qs, [kv_block(pl.multiple_of((qi - 1 - n) * tk, tk), None)], uu)
                return n + 1, alive()

            _, live = lax.while_loop(kv_cond, kv_step, (jnp.int32(1), alive()))

        if has_ctx:
            @pl.when(live)
            def _():
                cw = kc_ref.shape[1]
                ctx_mask = lax.broadcasted_iota(jnp.int32, (tq, cw), 1) >= ctx_pad
                sweep(qs, [([kc_ref[0, :, hcols(hh)] for hh in range(heads)],
                            [vc_ref[0, :, hcols(hh)] for hh in range(heads)], ctx_mask)],
                      uc_ref[...])

        g = g_ref[...]
        for hh in range(heads):
            o_ref[0, pl.ds(row0, tq), hcols(hh)] = _rms(acc_ref[hh], g[:, hcols(hh)]).astype(BF16)

    q_block(0)
    if nq > 1:
        def q_step(qi, carry):
            q_block(qi)
            return carry
        lax.fori_loop(1, nq, q_step, 0)


def _attention(q, k, v, ctx, g_ao, *, tq, heads, own_pad, ctx_pad):
    b, seq, width = q.shape
    hw = heads * HEAD_DIM
    blk = pl.BlockSpec((1, seq, hw), lambda bi, hi: (bi, 0, hi))
    in_specs = [blk, blk, blk]
    args = [q, k, v]
    scratch = [pltpu.VMEM((2 * tq, tq), BF16)]
    if ctx is not None:
        kc, vc = ctx
        cblk = pl.BlockSpec((1, kc.shape[1], hw), lambda bi, hi: (0, 0, hi))
        in_specs += [cblk, cblk]
        args += [kc, vc]
        scratch.append(pltpu.VMEM((2 * kc.shape[1], kc.shape[1]), BF16))
    in_specs.append(pl.BlockSpec((1, hw), lambda bi, hi: (0, hi)))
    args.append(g_ao)
    kern = functools.partial(_attn_kernel, seq=seq, tq=tq, own_pad=own_pad,
                             ctx_pad=ctx_pad if ctx is not None else None)
    return pl.pallas_call(
        kern,
        grid=(b, width // hw),
        in_specs=in_specs,
        out_specs=blk,
        out_shape=jax.ShapeDtypeStruct((b, seq, width), BF16),
        scratch_shapes=scratch + [
            pltpu.VMEM((heads, tq, HEAD_DIM), F32),
            pltpu.VMEM((heads, tq, HEAD_DIM), F32),
        ],
        compiler_params=pltpu.CompilerParams(
            dimension_semantics=("arbitrary", "arbitrary"),
            vmem_limit_bytes=VMEM_LIMIT),
        name="sb_attention",
    )(*args)


def _out_proj_kernel(h_ref, yc_ref, ya_ref, w_ref, o_ref):
    half = yc_ref.shape[1]
    o_ref[...] = (h_ref[...]
                  + jnp.dot(yc_ref[...], w_ref[0:half, :], preferred_element_type=F32)
                  + jnp.dot(ya_ref[...], w_ref[half:, :], preferred_element_type=F32))


def _out_proj(h, yc, ya, w_out, layer, *, tm):
    m, d = h.shape
    half = yc.shape[1]
    row = pl.BlockSpec((tm, d), lambda i: (i, 0))
    yrow = pl.BlockSpec((tm, half), lambda i: (i, 0))
    return pl.pallas_call(
        _out_proj_kernel,
        grid=(m // tm,),
        in_specs=[row, yrow, yrow,
                  pl.BlockSpec((None,) + w_out.shape[1:], lambda i: (layer, 0, 0))],
        out_specs=row,
        out_shape=jax.ShapeDtypeStruct((m, d), F32),
        compiler_params=pltpu.CompilerParams(
            dimension_semantics=("arbitrary",), vmem_limit_bytes=VMEM_LIMIT),
        name="out_proj",
    )(h, yc, ya, w_out)


def _mlp_kernel(h_ref, g_ref, w1_ref, w2_ref, o_ref, hn_ref):
    f = pl.program_id(1)

    @pl.when(f == 0)
    def _():
        x = h_ref[...]
        hn_ref[...] = _rms(x, g_ref[...]).astype(BF16)
        o_ref[...] = x

    hid = jnp.dot(hn_ref[...], w1_ref[...], preferred_element_type=F32)
    hid = jnp.square(jnp.maximum(hid, 0.0)).astype(BF16)
    o_ref[...] += jnp.dot(hid, w2_ref[...], preferred_element_type=F32)


def _mlp(h, g_mlp, w1, w2, layer, *, tm, tf):
    m, d = h.shape
    d_ff = w1.shape[2]
    row = pl.BlockSpec((tm, d), lambda i, f: (i, 0))
    return pl.pallas_call(
        _mlp_kernel,
        grid=(m // tm, d_ff // tf),
        in_specs=[row,
                  pl.BlockSpec((1, d), lambda i, f: (0, 0)),
                  pl.BlockSpec((None, d, tf), lambda i, f: (layer, 0, f)),
                  pl.BlockSpec((None, tf, d), lambda i, f: (layer, f, 0))],
        out_specs=row,
        out_shape=jax.ShapeDtypeStruct((m, d), F32),
        scratch_shapes=[pltpu.VMEM((tm, d), BF16)],
        compiler_params=pltpu.CompilerParams(
            dimension_semantics=("arbitrary", "arbitrary"), vmem_limit_bytes=VMEM_LIMIT),
        name="mlp",
    )(h, g_mlp, w1, w2)


def _tiles(seq):
    return dict(tm_in=min(512, seq), tm_out=min(512, seq), tm_mlp=min(1024, seq), tf=1024,
                tq=min(256, seq), heads=4)


def kernel(x, meta_tokens, g_mix, w_in, conv_w, g_q, g_k, g_conv_out, g_attn_out, w_out,
           g_mlp, w_mlp_in, w_mlp_out):
    b, seq, d = x.shape
    depth = w_in.shape[0]
    n_meta = meta_tokens.shape[0]
    width = conv_w.shape[2]
    assert n_meta <= META_ROWS and seq % META_ROWS == 0
    cfg = _tiles(seq)
    meta_pad = META_ROWS - n_meta

    h = x.reshape(b * seq, d)
    hm = jnp.pad(meta_tokens.astype(x.dtype), ((meta_pad, 0), (0, 0)))
    zero_tail = jnp.zeros((SUBLANES, width), F32)
    w_in_b = w_in.astype(BF16)
    w_out_b = w_out.astype(BF16)
    w1_b = w_mlp_in.astype(BF16)
    w2_b = w_mlp_out.astype(BF16)

    for i in range(depth):
        g_mix_i = g_mix[i].reshape(1, d)
        g_co_i = g_conv_out[i].reshape(1, width)
        g_ao_i = g_attn_out[i].reshape(1, width)
        g_mlp_i = g_mlp[i].reshape(1, d)
        g_qk_i = jnp.stack([g_q[i] * (HEAD_DIM ** -0.5 * LOG2E), g_k[i]])

        ycm, qm, km, vm, tail_m = _in_proj(hm, g_mix_i, w_in_b, i, conv_w[i], g_co_i, g_qk_i,
                                           zero_tail, tm=META_ROWS, tiles_per_seq=1)
        yc, q, k, v, _ = _in_proj(h, g_mix_i, w_in_b, i, conv_w[i], g_co_i, g_qk_i, tail_m[0],
                                  tm=cfg["tm_in"], tiles_per_seq=seq // cfg["tm_in"])
        ya = _attention(q.reshape(b, seq, width), k.reshape(b, seq, width),
                        v.reshape(b, seq, width), (km[None], vm[None]), g_ao_i,
                        tq=cfg["tq"], heads=cfg["heads"], own_pad=0, ctx_pad=meta_pad)
        h = _out_proj(h, yc, ya.reshape(b * seq, width), w_out_b, i, tm=cfg["tm_out"])
        h = _mlp(h, g_mlp_i, w1_b, w2_b, i, tm=cfg["tm_mlp"], tf=cfg["tf"])

        if i + 1 < depth:
            yam = _attention(qm[None], km[None], vm[None], None, g_ao_i,
                             tq=META_ROWS, heads=cfg["heads"], own_pad=meta_pad, ctx_pad=None)
            hm = _out_proj(hm, ycm, yam[0], w_out_b, i, tm=META_ROWS)
            hm = _mlp(hm, g_mlp_i, w1_b, w2_b, i, tm=META_ROWS, tf=cfg["tf"])

    return h.reshape(b, seq, d)
```

```python
import functools

import jax
import jax.numpy as jnp
from jax import lax
from jax.experimental import pallas as pl
from jax.experimental.pallas import tpu as pltpu

F32 = jnp.float32
BF16 = jnp.bfloat16

EPS = 1e-6
HEAD_DIM = 128
CONV_K = 3
SUBLANES = 8
BF16_ROWS = 16
META_ROWS = 128
COL_CHUNK = 2 * HEAD_DIM
VMEM_LIMIT = 60 * 1024 * 1024
NEG_BIG = -1e30
LOG2E = 1.4426950408889634
DEAD_LOG2 = -160.0


def _rms(x, g):
    ms = jnp.mean(x * x, axis=-1, keepdims=True)
    return x * lax.rsqrt(ms + EPS) * g


def _in_proj_kernel(x_ref, g_ref, w_ref, cw_ref, gco_ref, gqk_ref, ctx_ref, *rest,
                    tm, tiles_per_seq, n_riders):
    rider_in = rest[:n_riders]
    yc_ref, q_ref, k_ref, v_ref, tail_ref = rest[n_riders:n_riders + 5]
    rider_out = rest[n_riders + 5:2 * n_riders + 5]
    hn_ref, cu_ref, carry_ref = rest[2 * n_riders + 5:]
    step = pl.program_id(0)
    cc = COL_CHUNK
    width = yc_ref.shape[1]
    n_chunks = width // cc

    def cast_riders():
        for src, dst in zip(rider_in, rider_out):
            dst[...] = src[...].astype(BF16)

    @pl.when(step == 0)
    def _():
        hn_ref[0] = _rms(x_ref[...], g_ref[...]).astype(BF16)
        cast_riders()

    @pl.when(jnp.logical_and(step > 0, ((step - 1) % tiles_per_seq) == 0))
    def _():
        carry_ref[...] = ctx_ref[...]

    @pl.when(step > 0)
    def _tile():
        slot = (step - 1) % 2

        def proj(section, c):
            col = section * width + c * cc
            return jnp.dot(hn_ref[slot], w_ref[:, col:col + cc], preferred_element_type=F32)

        for c in range(n_chunks):
            cs = slice(c * cc, (c + 1) * cc)
            gate_b = proj(0, c)
            cu = proj(1, c) * proj(2, c)
            if c == 0:
                hn_ref[1 - slot] = _rms(x_ref[...], g_ref[...]).astype(BF16)
                cast_riders()
            cu_ref[c, SUBLANES:SUBLANES + tm, :] = cu
            cu_ref[c, 0:SUBLANES, :] = carry_ref[:, cs]
            tail = cu[tm - SUBLANES:tm, :]
            carry_ref[:, cs] = tail
            tail_ref[0, :, cs] = tail
            c1 = cu_ref[c, SUBLANES - 1:SUBLANES - 1 + tm, :]
            c2 = cu_ref[c, SUBLANES - 2:SUBLANES - 2 + tm, :]
            w = cw_ref[:, cs]
            y = gate_b * (w[0:1, :] * c2 + w[1:2, :] * c1 + w[2:3, :] * cu)
            for s in range(c * cc, (c + 1) * cc, HEAD_DIM):
                sl = slice(s, s + HEAD_DIM)
                yc_ref[:, sl] = _rms(y[:, s - c * cc:s - c * cc + HEAD_DIM],
                                     gco_ref[:, sl]).astype(BF16)

        gqk = gqk_ref[...]
        for section, o_ref, gain in ((3, q_ref, gqk[0:1, :]), (4, k_ref, gqk[1:2, :])):
            for c in range(n_chunks):
                acc = proj(section, c)
                for s in range(0, cc, HEAD_DIM):
                    o_ref[:, c * cc + s:c * cc + s + HEAD_DIM] = _rms(
                        acc[:, s:s + HEAD_DIM], gain).astype(BF16)
        for c in range(n_chunks):
            v_ref[:, c * cc:(c + 1) * cc] = proj(5, c).astype(BF16)


def _can_ride(a, n_tiles):
    return a.shape[1] % (n_tiles * BF16_ROWS) == 0


def _in_proj(h, g_mix, w_all, layer, conv_w, g_co, g_qk, ctx_tail, riders=(), *,
             tm, tiles_per_seq):
    m, d = h.shape
    width = conv_w.shape[1]
    n_tiles = m // tm
    assert w_all.shape[2] == 6 * width and width % COL_CHUNK == 0 and m % tm == 0
    assert all(_can_ride(a, n_tiles) for a in riders)

    kern = functools.partial(_in_proj_kernel, tm=tm, tiles_per_seq=tiles_per_seq,
                             n_riders=len(riders))
    act = jax.ShapeDtypeStruct((m, width), BF16)
    row = pl.BlockSpec((tm, width), lambda s: (jnp.maximum(s - 1, 0), 0))
    const = lambda s: (0, 0)
    slab = lambda s: jnp.minimum(s, n_tiles - 1)
    return pl.pallas_call(
        kern,
        grid=(n_tiles + 1,),
        in_specs=[
            pl.BlockSpec((tm, d), lambda s: (slab(s), 0)),
            pl.BlockSpec((1, d), const),
            pl.BlockSpec((None, d, 6 * width), lambda s: (layer, 0, 0),
                         pipeline_mode=pl.Buffered(1)),
            pl.BlockSpec((CONV_K, width), const),
            pl.BlockSpec((1, width), const),
            pl.BlockSpec((2, HEAD_DIM), const),
            pl.BlockSpec((SUBLANES, width), const),
        ] + [pl.BlockSpec((None, a.shape[1] // n_tiles, a.shape[2]),
                          lambda s: (layer, slab(s), 0)) for a in riders],
        out_specs=[row, row, row, row,
                   pl.BlockSpec((1, SUBLANES, width), lambda s: (jnp.maximum(s - 1, 0), 0, 0))]
        + [pl.BlockSpec((a.shape[1] // n_tiles, a.shape[2]), lambda s: (slab(s), 0))
           for a in riders],
        out_shape=[act, act, act, act,
                   jax.ShapeDtypeStruct((n_tiles, SUBLANES, width), F32)]
        + [jax.ShapeDtypeStruct(a.shape[1:], BF16) for a in riders],
        scratch_shapes=[
            pltpu.VMEM((2, tm, d), BF16),
            pltpu.VMEM((width // COL_CHUNK, tm + SUBLANES, COL_CHUNK), F32),
            pltpu.VMEM((SUBLANES, width), F32),
        ],
        compiler_params=pltpu.CompilerParams(
            dimension_semantics=("arbitrary",),
            vmem_limit_bytes=VMEM_LIMIT),
        name="in_proj",
    )(h, g_mix, w_all, conv_w, g_co, g_qk, ctx_tail, *riders)


def _neg_abs(x):
    bits = lax.bitcast_convert_type(x, jnp.int32) | jnp.int32(-2 ** 31)
    return lax.bitcast_convert_type(bits, F32)


def _neg_suffix_matrix(n):
    rr = lax.broadcasted_iota(jnp.int32, (2 * n, n), 0)
    cc = lax.broadcasted_iota(jnp.int32, (2 * n, n), 1)
    rr = jnp.where(rr >= n, rr - n, rr)
    return jnp.where(rr >= cc, -1.0, 0.0).astype(BF16)


def _attn_kernel(*refs, seq, tq, own_pad, ctx_pad):
    has_ctx = ctx_pad is not None
    if has_ctx:
        (q_ref, k_ref, v_ref, kc_ref, vc_ref, g_ref, o_ref,
         uu_ref, uc_ref, acc_ref, r_ref) = refs
    else:
        q_ref, k_ref, v_ref, g_ref, o_ref, uu_ref, acc_ref, r_ref = refs
    tk = tq
    nq = seq // tq
    heads = q_ref.shape[2] // HEAD_DIM
    nt = (((1,), (1,)), ((), ()))

    uu_ref[...] = _neg_suffix_matrix(tk)
    if has_ctx:
        uc_ref[...] = _neg_suffix_matrix(kc_ref.shape[1])
    rr = lax.broadcasted_iota(jnp.int32, (tq, tk), 0)
    cc = lax.broadcasted_iota(jnp.int32, (tq, tk), 1)
    diag_mask = cc < rr
    if own_pad:
        diag_mask = jnp.logical_and(diag_mask, cc >= own_pad)

    def hcols(hh):
        return slice(hh * HEAD_DIM, (hh + 1) * HEAD_DIM)

    def sweep(qs, blocks, uu):
        hs = range(heads)
        zs = [[lax.dot_general(qs[hh], kbs[hh], nt, preferred_element_type=F32) for hh in hs]
              for kbs, _, _ in blocks]
        css = []
        for zb, (_, _, mask) in zip(zs, blocks):
            row = []
            for z in zb:
                nl = jnp.maximum(z, 0.0) + jnp.log2(1.0 + jnp.exp2(_neg_abs(z)))
                if mask is not None:
                    nl = jnp.where(mask, nl, 0.0)
                hi = nl.astype(BF16)
                lo = (nl - hi.astype(F32)).astype(BF16)
                row.append(jnp.dot(jnp.concatenate([hi, lo], axis=1), uu,
                                   preferred_element_type=F32))
            css.append(row)
        rs = [r_ref[hh] for hh in hs]
        avs = [None] * heads
        for zb, cb, (kbs, vbs, mask) in zip(zs, css, blocks):
            width = kbs[0].shape[0]
            for hh in hs:
                t = zb[hh] + cb[hh] + jnp.concatenate([rs[hh]] * (width // HEAD_DIM), axis=1)
                if mask is not None:
                    t = jnp.where(mask, t, NEG_BIG)
                a = jnp.exp2(t).astype(BF16)
                av = jnp.dot(a, vbs[hh], preferred_element_type=F32)
                avs[hh] = av if avs[hh] is None else avs[hh] + av
                rs[hh] = rs[hh] + jnp.broadcast_to(cb[hh][:, 0:1], (tq, HEAD_DIM))
        for hh in hs:
            r_ref[hh] = rs[hh]
            acc_ref[hh] += avs[hh]

    def kv_block(c0, mask):
        return ([k_ref[0, pl.ds(c0, tk), hcols(hh)] for hh in range(heads)],
                [v_ref[0, pl.ds(c0, tk), hcols(hh)] for hh in range(heads)], mask)

    def alive():
        return jnp.max(r_ref[...]) > DEAD_LOG2

    def q_block(qi):
        first = isinstance(qi, int) and qi == 0
        row0 = qi * tq if isinstance(qi, int) else pl.multiple_of(qi * tq, tq)
        qs = [q_ref[0, pl.ds(row0, tq), hcols(hh)] for hh in range(heads)]
        acc_ref[...] = jnp.zeros_like(acc_ref)
        r_ref[...] = jnp.zeros_like(r_ref)
        uu = uu_ref[...]

        if first:
            sweep(qs, [kv_block(row0, diag_mask)], uu)
            live = alive()
        else:
            sweep(qs, [kv_block(row0, diag_mask),
                       kv_block(pl.multiple_of(row0 - tk, tk), None)], uu)

            def kv_cond(carry):
                return jnp.logical_and(carry[0] < qi, carry[1])

            def kv_step(carry):
                n = carry[0]
                sweep(qs, [kv_block(pl.multiple_of((qi - 1 - n) * tk, tk), None)], uu)
                return n + 1, alive()

            _, live = lax.while_loop(kv_cond, kv_step, (jnp.int32(1), alive()))

        if has_ctx:
            @pl.when(live)
            def _():
                cw = kc_ref.shape[1]
                ctx_mask = lax.broadcasted_iota(jnp.int32, (tq, cw), 1) >= ctx_pad
                sweep(qs, [([kc_ref[0, :, hcols(hh)] for hh in range(heads)],
                            [vc_ref[0, :, hcols(hh)] for hh in range(heads)], ctx_mask)],
                      uc_ref[...])

        g = g_ref[...]
        for hh in range(heads):
            o_ref[0, pl.ds(row0, tq), hcols(hh)] = _rms(acc_ref[hh], g[:, hcols(hh)]).astype(BF16)

    q_block(0)
    if nq > 1:
        def q_step(qi, carry):
            q_block(qi)
            return carry
        lax.fori_loop(1, nq, q_step, 0)


def _attention(q, k, v, ctx, g_ao, *, tq, heads, own_pad, ctx_pad):
    b, seq, width = q.shape
    hw = heads * HEAD_DIM
    blk = pl.BlockSpec((1, seq, hw), lambda bi, hi: (bi, 0, hi))
    in_specs = [blk, blk, blk]
    args = [q, k, v]
    scratch = [pltpu.VMEM((2 * tq, tq), BF16)]
    if ctx is not None:
        kc, vc = ctx
        cblk = pl.BlockSpec((1, kc.shape[1], hw), lambda bi, hi: (0, 0, hi))
        in_specs += [cblk, cblk]
        args += [kc, vc]
        scratch.append(pltpu.VMEM((2 * kc.shape[1], kc.shape[1]), BF16))
    in_specs.append(pl.BlockSpec((1, hw), lambda bi, hi: (0, hi)))
    args.append(g_ao)
    kern = functools.partial(_attn_kernel, seq=seq, tq=tq, own_pad=own_pad,
                             ctx_pad=ctx_pad if ctx is not None else None)
    return pl.pallas_call(
        kern,
        grid=(b, width // hw),
        in_specs=in_specs,
        out_specs=blk,
        out_shape=jax.ShapeDtypeStruct((b, seq, width), BF16),
        scratch_shapes=scratch + [
            pltpu.VMEM((heads, tq, HEAD_DIM), F32),
            pltpu.VMEM((heads, tq, HEAD_DIM), F32),
        ],
        compiler_params=pltpu.CompilerParams(
            dimension_semantics=("arbitrary", "arbitrary"),
            vmem_limit_bytes=VMEM_LIMIT),
        name="sb_attention",
    )(*args)


def _out_proj_kernel(h_ref, yc_ref, ya_ref, w_ref, o_ref):
    half = yc_ref.shape[1]
    o_ref[...] = (h_ref[...]
                  + jnp.dot(yc_ref[...], w_ref[0:half, :], preferred_element_type=F32)
                  + jnp.dot(ya_ref[...], w_ref[half:, :], preferred_element_type=F32))


def _out_proj(h, yc, ya, w_out, layer, *, tm):
    m, d = h.shape
    half = yc.shape[1]
    row = pl.BlockSpec((tm, d), lambda i: (i, 0))
    yrow = pl.BlockSpec((tm, half), lambda i: (i, 0))
    return pl.pallas_call(
        _out_proj_kernel,
        grid=(m // tm,),
        in_specs=[row, yrow, yrow,
                  pl.BlockSpec((None,) + w_out.shape[1:], lambda i: (layer, 0, 0))],
        out_specs=row,
        out_shape=jax.ShapeDtypeStruct((m, d), F32),
        compiler_params=pltpu.CompilerParams(
            dimension_semantics=("arbitrary",), vmem_limit_bytes=VMEM_LIMIT),
        name="out_proj",
    )(h, yc, ya, w_out)


def _mlp_kernel(h_ref, g_ref, w1_ref, w2_ref, o_ref, hn_ref):
    f = pl.program_id(1)

    @pl.when(f == 0)
    def _():
        x = h_ref[...]
        hn_ref[...] = _rms(x, g_ref[...]).astype(BF16)
        o_ref[...] = x

    hid = jnp.dot(hn_ref[...], w1_ref[...], preferred_element_type=F32)
    hid = jnp.square(jnp.maximum(hid, 0.0)).astype(BF16)
    o_ref[...] += jnp.dot(hid, w2_ref[...], preferred_element_type=F32)


def _mlp(h, g_mlp, w1, w2, *, tm, tf):
    m, d = h.shape
    d_ff = w1.shape[1]
    row = pl.BlockSpec((tm, d), lambda i, f: (i, 0))
    return pl.pallas_call(
        _mlp_kernel,
        grid=(m // tm, d_ff // tf),
        in_specs=[row,
                  pl.BlockSpec((1, d), lambda i, f: (0, 0)),
                  pl.BlockSpec((d, tf), lambda i, f: (0, f)),
                  pl.BlockSpec((tf, d), lambda i, f: (f, 0))],
        out_specs=row,
        out_shape=jax.ShapeDtypeStruct((m, d), F32),
        scratch_shapes=[pltpu.VMEM((tm, d), BF16)],
        compiler_params=pltpu.CompilerParams(
            dimension_semantics=("arbitrary", "arbitrary"), vmem_limit_bytes=VMEM_LIMIT),
        name="mlp",
    )(h, g_mlp, w1, w2)


def _tiles(seq):
    return dict(tm_in=min(512, seq), tm_out=min(512, seq), tm_mlp=min(1024, seq), tf=1024,
                tq=min(256, seq), heads=4)


def kernel(x, meta_tokens, g_mix, w_in, conv_w, g_q, g_k, g_conv_out, g_attn_out, w_out,
           g_mlp, w_mlp_in, w_mlp_out):
    b, seq, d = x.shape
    depth = w_in.shape[0]
    n_meta = meta_tokens.shape[0]
    width = conv_w.shape[2]
    assert n_meta <= META_ROWS and seq % META_ROWS == 0
    cfg = _tiles(seq)
    meta_pad = META_ROWS - n_meta

    h = x.reshape(b * seq, d)
    hm = jnp.pad(meta_tokens.astype(x.dtype), ((meta_pad, 0), (0, 0)))
    zero_tail = jnp.zeros((SUBLANES, width), F32)
    w_in_b = w_in.astype(BF16)
    w_out_b = w_out.astype(BF16)
    mlp_w = (w_mlp_in, w_mlp_out)
    ride = all(_can_ride(a, b * seq // cfg["tm_in"]) for a in mlp_w)

    for i in range(depth):
        g_mix_i = g_mix[i].reshape(1, d)
        g_co_i = g_conv_out[i].reshape(1, width)
        g_ao_i = g_attn_out[i].reshape(1, width)
        g_mlp_i = g_mlp[i].reshape(1, d)
        g_qk_i = jnp.stack([g_q[i] * (HEAD_DIM ** -0.5 * LOG2E), g_k[i]])

        ycm, qm, km, vm, tail_m = _in_proj(hm, g_mix_i, w_in_b, i, conv_w[i], g_co_i, g_qk_i,
                                           zero_tail, tm=META_ROWS, tiles_per_seq=1)
        yc, q, k, v, _, *w_mlp_b = _in_proj(h, g_mix_i, w_in_b, i, conv_w[i], g_co_i, g_qk_i,
                                            tail_m[0], mlp_w if ride else (),
                                            tm=cfg["tm_in"], tiles_per_seq=seq // cfg["tm_in"])
        w1_b, w2_b = w_mlp_b if ride else (a[i].astype(BF16) for a in mlp_w)
        ya = _attention(q.reshape(b, seq, width), k.reshape(b, seq, width),
                        v.reshape(b, seq, width), (km[None], vm[None]), g_ao_i,
                        tq=cfg["tq"], heads=cfg["heads"], own_pad=0, ctx_pad=meta_pad)
        h = _out_proj(h, yc, ya.reshape(b * seq, width), w_out_b, i, tm=cfg["tm_out"])
        h = _mlp(h, g_mlp_i, w1_b, w2_b, tm=cfg["tm_mlp"], tf=cfg["tf"])

        if i + 1 < depth:
            yam = _attention(qm[None], km[None], vm[None], None, g_ao_i,
                             tq=META_ROWS, heads=cfg["heads"], own_pad=meta_pad, ctx_pad=None)
            hm = _out_proj(hm, ycm, yam[0], w_out_b, i, tm=META_ROWS)
            hm = _mlp(hm, g_mlp_i, w1_b, w2_b, tm=META_ROWS, tf=cfg["tf"])

    return h.reshape(b, seq, d)
```

```python
import functools

import jax
import jax.numpy as jnp
from jax import lax
from jax.experimental import pallas as pl
from jax.experimental.pallas import tpu as pltpu

F32 = jnp.float32
BF16 = jnp.bfloat16

EPS = 1e-6
HEAD_DIM = 128
CONV_K = 3
SUBLANES = 8
BF16_ROWS = 16
META_ROWS = 128
COL_CHUNK = 2 * HEAD_DIM
MLP_FIRST_SUB_ROWS = 256
VMEM_LIMIT = 60 * 1024 * 1024
NEG_BIG = -1e30
LOG2E = 1.4426950408889634
DEAD_LOG2 = -160.0


def _rms(x, g):
    ms = jnp.mean(x * x, axis=-1, keepdims=True)
    return x * lax.rsqrt(ms + EPS) * g


def _in_proj_kernel(x_ref, g_ref, w_ref, cw_ref, gco_ref, gqk_ref, ctx_ref, *rest,
                    tm, tiles_per_seq, n_riders):
    rider_in = rest[:n_riders]
    yc_ref, q_ref, k_ref, v_ref, tail_ref = rest[n_riders:n_riders + 5]
    rider_out = rest[n_riders + 5:2 * n_riders + 5]
    hn_ref, cu_ref, carry_ref = rest[2 * n_riders + 5:]
    step = pl.program_id(0)
    cc = COL_CHUNK
    width = yc_ref.shape[1]
    n_chunks = width // cc

    def cast_riders():
        for src, dst in zip(rider_in, rider_out):
            dst[...] = src[...].astype(BF16)

    @pl.when(step == 0)
    def _():
        hn_ref[0] = _rms(x_ref[...], g_ref[...]).astype(BF16)
        cast_riders()

    @pl.when(jnp.logical_and(step > 0, ((step - 1) % tiles_per_seq) == 0))
    def _():
        carry_ref[...] = ctx_ref[...]

    @pl.when(step > 0)
    def _tile():
        slot = (step - 1) % 2

        def proj(section, c):
            col = section * width + c * cc
            return jnp.dot(hn_ref[slot], w_ref[:, col:col + cc], preferred_element_type=F32)

        for c in range(n_chunks):
            cs = slice(c * cc, (c + 1) * cc)
            gate_b = proj(0, c)
            cu = proj(1, c) * proj(2, c)
            if c == 0:
                hn_ref[1 - slot] = _rms(x_ref[...], g_ref[...]).astype(BF16)
                cast_riders()
            cu_ref[c, SUBLANES:SUBLANES + tm, :] = cu
            cu_ref[c, 0:SUBLANES, :] = carry_ref[:, cs]
            tail = cu[tm - SUBLANES:tm, :]
            carry_ref[:, cs] = tail
            tail_ref[0, :, cs] = tail
            c1 = cu_ref[c, SUBLANES - 1:SUBLANES - 1 + tm, :]
            c2 = cu_ref[c, SUBLANES - 2:SUBLANES - 2 + tm, :]
            w = cw_ref[:, cs]
            y = gate_b * (w[0:1, :] * c2 + w[1:2, :] * c1 + w[2:3, :] * cu)
            for s in range(c * cc, (c + 1) * cc, HEAD_DIM):
                sl = slice(s, s + HEAD_DIM)
                yc_ref[:, sl] = _rms(y[:, s - c * cc:s - c * cc + HEAD_DIM],
                                     gco_ref[:, sl]).astype(BF16)

        gqk = gqk_ref[...]
        for section, o_ref, gain in ((3, q_ref, gqk[0:1, :]), (4, k_ref, gqk[1:2, :])):
            for c in range(n_chunks):
                acc = proj(section, c)
                for s in range(0, cc, HEAD_DIM):
                    o_ref[:, c * cc + s:c * cc + s + HEAD_DIM] = _rms(
                        acc[:, s:s + HEAD_DIM], gain).astype(BF16)
        for c in range(n_chunks):
            v_ref[:, c * cc:(c + 1) * cc] = proj(5, c).astype(BF16)


def _can_ride(a, n_tiles):
    return a.shape[1] % (n_tiles * BF16_ROWS) == 0


def _in_proj(h, g_mix, w, conv_w, g_co, g_qk, ctx_tail, riders=(), *, tm, tiles_per_seq):
    m, d = h.shape
    width = conv_w.shape[1]
    n_tiles = m // tm
    assert w.shape == (d, 6 * width) and width % COL_CHUNK == 0 and m % tm == 0
    assert all(_can_ride(a, n_tiles) for a, _ in riders)

    kern = functools.partial(_in_proj_kernel, tm=tm, tiles_per_seq=tiles_per_seq,
                             n_riders=len(riders))
    act = jax.ShapeDtypeStruct((m, width), BF16)
    row = pl.BlockSpec((tm, width), lambda s: (jnp.maximum(s - 1, 0), 0))
    const = lambda s: (0, 0)
    slab = lambda s: jnp.minimum(s, n_tiles - 1)
    return pl.pallas_call(
        kern,
        grid=(n_tiles + 1,),
        in_specs=[
            pl.BlockSpec((tm, d), lambda s: (slab(s), 0)),
            pl.BlockSpec((1, d), const),
            pl.BlockSpec(w.shape, const, pipeline_mode=pl.Buffered(1)),
            pl.BlockSpec((CONV_K, width), const),
            pl.BlockSpec((1, width), const),
            pl.BlockSpec((2, HEAD_DIM), const),
            pl.BlockSpec((SUBLANES, width), const),
        ] + [pl.BlockSpec((None, a.shape[1] // n_tiles, a.shape[2]),
                          functools.partial(lambda layer, s: (layer, slab(s), 0), layer))
             for a, layer in riders],
        out_specs=[row, row, row, row,
                   pl.BlockSpec((1, SUBLANES, width), lambda s: (jnp.maximum(s - 1, 0), 0, 0))]
        + [pl.BlockSpec((a.shape[1] // n_tiles, a.shape[2]), lambda s: (slab(s), 0))
           for a, _ in riders],
        out_shape=[act, act, act, act,
                   jax.ShapeDtypeStruct((n_tiles, SUBLANES, width), F32)]
        + [jax.ShapeDtypeStruct(a.shape[1:], BF16) for a, _ in riders],
        scratch_shapes=[
            pltpu.VMEM((2, tm, d), BF16),
            pltpu.VMEM((width // COL_CHUNK, tm + SUBLANES, COL_CHUNK), F32),
            pltpu.VMEM((SUBLANES, width), F32),
        ],
        compiler_params=pltpu.CompilerParams(
            dimension_semantics=("arbitrary",),
            vmem_limit_bytes=VMEM_LIMIT),
        name="in_proj",
    )(h, g_mix, w, conv_w, g_co, g_qk, ctx_tail, *[a for a, _ in riders])


def _neg_abs(x):
    bits = lax.bitcast_convert_type(x, jnp.int32) | jnp.int32(-2 ** 31)
    return lax.bitcast_convert_type(bits, F32)


def _neg_suffix_matrix(n):
    rr = lax.broadcasted_iota(jnp.int32, (2 * n, n), 0)
    cc = lax.broadcasted_iota(jnp.int32, (2 * n, n), 1)
    rr = jnp.where(rr >= n, rr - n, rr)
    return jnp.where(rr >= cc, -1.0, 0.0).astype(BF16)


def _attn_kernel(*refs, seq, tq, own_pad, ctx_pad):
    has_ctx = ctx_pad is not None
    if has_ctx:
        (q_ref, k_ref, v_ref, kc_ref, vc_ref, g_ref, o_ref,
         uu_ref, uc_ref, acc_ref, r_ref) = refs
    else:
        q_ref, k_ref, v_ref, g_ref, o_ref, uu_ref, acc_ref, r_ref = refs
    tk = tq
    nq = seq // tq
    heads = q_ref.shape[2] // HEAD_DIM
    nt = (((1,), (1,)), ((), ()))

    uu_ref[...] = _neg_suffix_matrix(tk)
    if has_ctx:
        uc_ref[...] = _neg_suffix_matrix(kc_ref.shape[1])
    rr = lax.broadcasted_iota(jnp.int32, (tq, tk), 0)
    cc = lax.broadcasted_iota(jnp.int32, (tq, tk), 1)
    diag_mask = cc < rr
    if own_pad:
        diag_mask = jnp.logical_and(diag_mask, cc >= own_pad)

    def hcols(hh):
        return slice(hh * HEAD_DIM, (hh + 1) * HEAD_DIM)

    def sweep(qs, blocks, uu):
        hs = range(heads)
        zs = [[lax.dot_general(qs[hh], kbs[hh], nt, preferred_element_type=F32) for hh in hs]
              for kbs, _, _ in blocks]
        css = []
        for zb, (_, _, mask) in zip(zs, blocks):
            row = []
            for z in zb:
                nl = jnp.maximum(z, 0.0) + jnp.log2(1.0 + jnp.exp2(_neg_abs(z)))
                if mask is not None:
                    nl = jnp.where(mask, nl, 0.0)
                hi = nl.astype(BF16)
                lo = (nl - hi.astype(F32)).astype(BF16)
                row.append(jnp.dot(jnp.concatenate([hi, lo], axis=1), uu,
                                   preferred_element_type=F32))
            css.append(row)
        rs = [r_ref[hh] for hh in hs]
        avs = [None] * heads
        for zb, cb, (kbs, vbs, mask) in zip(zs, css, blocks):
            width = kbs[0].shape[0]
            for hh in hs:
                t = zb[hh] + cb[hh] + jnp.concatenate([rs[hh]] * (width // HEAD_DIM), axis=1)
                if mask is not None:
                    t = jnp.where(mask, t, NEG_BIG)
                a = jnp.exp2(t).astype(BF16)
                av = jnp.dot(a, vbs[hh], preferred_element_type=F32)
                avs[hh] = av if avs[hh] is None else avs[hh] + av
                rs[hh] = rs[hh] + jnp.broadcast_to(cb[hh][:, 0:1], (tq, HEAD_DIM))
        for hh in hs:
            r_ref[hh] = rs[hh]
            acc_ref[hh] += avs[hh]

    def kv_block(c0, mask):
        return ([k_ref[0, pl.ds(c0, tk), hcols(hh)] for hh in range(heads)],
                [v_ref[0, pl.ds(c0, tk), hcols(hh)] for hh in range(heads)], mask)

    def alive():
        return jnp.max(r_ref[...]) > DEAD_LOG2

    def q_block(qi):
        first = isinstance(qi, int) and qi == 0
        row0 = qi * tq if isinstance(qi, int) else pl.multiple_of(qi * tq, tq)
        qs = [q_ref[0, pl.ds(row0, tq), hcols(hh)] for hh in range(heads)]
        acc_ref[...] = jnp.zeros_like(acc_ref)
        r_ref[...] = jnp.zeros_like(r_ref)
        uu = uu_ref[...]

        if first:
            sweep(qs, [kv_block(row0, diag_mask)], uu)
            live = alive()
        else:
            sweep(qs, [kv_block(row0, diag_mask),
                       kv_block(pl.multiple_of(row0 - tk, tk), None)], uu)

            def kv_cond(carry):
                return jnp.logical_and(carry[0] < qi, carry[1])

            def kv_step(carry):
                n = carry[0]
                sweep(qs, [kv_block(pl.multiple_of((qi - 1 - n) * tk, tk), None)], uu)
                return n + 1, alive()

            _, live = lax.while_loop(kv_cond, kv_step, (jnp.int32(1), alive()))

        if has_ctx:
            @pl.when(live)
            def _():
                cw = kc_ref.shape[1]
                ctx_mask = lax.broadcasted_iota(jnp.int32, (tq, cw), 1) >= ctx_pad
                sweep(qs, [([kc_ref[0, :, hcols(hh)] for hh in range(heads)],
                            [vc_ref[0, :, hcols(hh)] for hh in range(heads)], ctx_mask)],
                      uc_ref[...])

        g = g_ref[...]
        for hh in range(heads):
            o_ref[0, pl.ds(row0, tq), hcols(hh)] = _rms(acc_ref[hh], g[:, hcols(hh)]).astype(BF16)

    q_block(0)
    if nq > 1:
        def q_step(qi, carry):
            q_block(qi)
            return carry
        lax.fori_loop(1, nq, q_step, 0)


def _attention(q, k, v, ctx, g_ao, *, tq, heads, own_pad, ctx_pad):
    b, seq, width = q.shape
    hw = heads * HEAD_DIM
    blk = pl.BlockSpec((1, seq, hw), lambda bi, hi: (bi, 0, hi))
    in_specs = [blk, blk, blk]
    args = [q, k, v]
    scratch = [pltpu.VMEM((2 * tq, tq), BF16)]
    if ctx is not None:
        kc, vc = ctx
        cblk = pl.BlockSpec((1, kc.shape[1], hw), lambda bi, hi: (0, 0, hi))
        in_specs += [cblk, cblk]
        args += [kc, vc]
        scratch.append(pltpu.VMEM((2 * kc.shape[1], kc.shape[1]), BF16))
    in_specs.append(pl.BlockSpec((1, hw), lambda bi, hi: (0, hi)))
    args.append(g_ao)
    kern = functools.partial(_attn_kernel, seq=seq, tq=tq, own_pad=own_pad,
                             ctx_pad=ctx_pad if ctx is not None else None)
    return pl.pallas_call(
        kern,
        grid=(b, width // hw),
        in_specs=in_specs,
        out_specs=blk,
        out_shape=jax.ShapeDtypeStruct((b, seq, width), BF16),
        scratch_shapes=scratch + [
            pltpu.VMEM((heads, tq, HEAD_DIM), F32),
            pltpu.VMEM((heads, tq, HEAD_DIM), F32),
        ],
        compiler_params=pltpu.CompilerParams(
            dimension_semantics=("arbitrary", "arbitrary"),
            vmem_limit_bytes=VMEM_LIMIT),
        name="sb_attention",
    )(*args)


def _out_proj_kernel(h_ref, yc_ref, ya_ref, w_ref, o_ref):
    half = yc_ref.shape[1]
    o_ref[...] = (h_ref[...]
                  + jnp.dot(yc_ref[...], w_ref[0:half, :], preferred_element_type=F32)
                  + jnp.dot(ya_ref[...], w_ref[half:, :], preferred_element_type=F32))


def _out_proj(h, yc, ya, w_out, *, tm):
    m, d = h.shape
    half = yc.shape[1]
    row = pl.BlockSpec((tm, d), lambda i: (i, 0))
    yrow = pl.BlockSpec((tm, half), lambda i: (i, 0))
    return pl.pallas_call(
        _out_proj_kernel,
        grid=(m // tm,),
        in_specs=[row, yrow, yrow, pl.BlockSpec(w_out.shape, lambda i: (0, 0))],
        out_specs=row,
        out_shape=jax.ShapeDtypeStruct((m, d), F32),
        compiler_params=pltpu.CompilerParams(
            dimension_semantics=("arbitrary",), vmem_limit_bytes=VMEM_LIMIT),
        name="out_proj",
    )(h, yc, ya, w_out)


def _mlp_kernel(h_ref, g_ref, w1_ref, w2_ref, o_ref, hn_ref, *, sub):
    f = pl.program_id(1)

    def ff_tile(hn):
        hid = jnp.dot(hn, w1_ref[...], preferred_element_type=F32)
        hid = jnp.square(jnp.maximum(hid, 0.0)).astype(BF16)
        return jnp.dot(hid, w2_ref[...], preferred_element_type=F32)

    @pl.when(f == 0)
    def _():
        for r in range(0, h_ref.shape[0], sub):
            rows = slice(r, r + sub)
            x = h_ref[rows, :]
            hn = _rms(x, g_ref[...]).astype(BF16)
            hn_ref[rows, :] = hn
            o_ref[rows, :] = x + ff_tile(hn)

    @pl.when(f > 0)
    def _():
        o_ref[...] += ff_tile(hn_ref[...])


def _mlp(h, g_mlp, w1, w2, *, tm, tf):
    m, d = h.shape
    d_ff = w1.shape[1]
    row = pl.BlockSpec((tm, d), lambda i, f: (i, 0))
    return pl.pallas_call(
        functools.partial(_mlp_kernel, sub=min(tm, MLP_FIRST_SUB_ROWS)),
        grid=(m // tm, d_ff // tf),
        in_specs=[row,
                  pl.BlockSpec((1, d), lambda i, f: (0, 0)),
                  pl.BlockSpec((d, tf), lambda i, f: (0, f)),
                  pl.BlockSpec((tf, d), lambda i, f: (f, 0))],
        out_specs=row,
        out_shape=jax.ShapeDtypeStruct((m, d), F32),
        scratch_shapes=[pltpu.VMEM((tm, d), BF16)],
        compiler_params=pltpu.CompilerParams(
            dimension_semantics=("arbitrary", "arbitrary"), vmem_limit_bytes=VMEM_LIMIT),
        name="mlp",
    )(h, g_mlp, w1, w2)


def _tiles(seq):
    return dict(tm_in=min(512, seq), tm_out=min(512, seq), tm_mlp=min(1024, seq), tf=1024,
                tq=min(256, seq), heads=4)


def kernel(x, meta_tokens, g_mix, w_in, conv_w, g_q, g_k, g_conv_out, g_attn_out, w_out,
           g_mlp, w_mlp_in, w_mlp_out):
    b, seq, d = x.shape
    depth = w_in.shape[0]
    n_meta = meta_tokens.shape[0]
    width = conv_w.shape[2]
    assert n_meta <= META_ROWS and seq % META_ROWS == 0
    cfg = _tiles(seq)
    meta_pad = META_ROWS - n_meta

    h = x.reshape(b * seq, d)
    hm = jnp.pad(meta_tokens.astype(x.dtype), ((meta_pad, 0), (0, 0)))
    zero_tail = jnp.zeros((SUBLANES, width), F32)
    later_w = (w_mlp_in, w_mlp_out, w_out)
    ride = all(_can_ride(a, b * seq // cfg["tm_in"]) for a in later_w + (w_in,))
    w_in_i = w_in[0].astype(BF16)

    for i in range(depth):
        g_mix_i = g_mix[i].reshape(1, d)
        g_co_i = g_conv_out[i].reshape(1, width)
        g_ao_i = g_attn_out[i].reshape(1, width)
        g_mlp_i = g_mlp[i].reshape(1, d)
        g_qk_i = jnp.stack([g_q[i] * (HEAD_DIM ** -0.5 * LOG2E), g_k[i]])
        last = i + 1 == depth
        riders = []
        if ride:
            riders = [(a, i) for a in later_w] + ([] if last else [(w_in, i + 1)])

        ycm, qm, km, vm, tail_m = _in_proj(hm, g_mix_i, w_in_i, conv_w[i], g_co_i, g_qk_i,
                                           zero_tail, tm=META_ROWS, tiles_per_seq=1)
        yc, q, k, v, _, *cast = _in_proj(h, g_mix_i, w_in_i, conv_w[i], g_co_i, g_qk_i,
                                         tail_m[0], riders,
                                         tm=cfg["tm_in"], tiles_per_seq=seq // cfg["tm_in"])
        if not ride:
            cast = [a[i].astype(BF16) for a in later_w] + ([] if last else [w_in[i + 1].astype(BF16)])
        w1_b, w2_b, w_out_b = cast[:3]
        ya = _attention(q.reshape(b, seq, width), k.reshape(b, seq, width),
                        v.reshape(b, seq, width), (km[None], vm[None]), g_ao_i,
                        tq=cfg["tq"], heads=cfg["heads"], own_pad=0, ctx_pad=meta_pad)
        h = _out_proj(h, yc, ya.reshape(b * seq, width), w_out_b, tm=cfg["tm_out"])
        h = _mlp(h, g_mlp_i, w1_b, w2_b, tm=cfg["tm_mlp"], tf=cfg["tf"])

        if not last:
            yam = _attention(qm[None], km[None], vm[None], None, g_ao_i,
                             tq=META_ROWS, heads=cfg["heads"], own_pad=meta_pad, ctx_pad=None)
            hm = _out_proj(hm, ycm, yam[0], w_out_b, tm=META_ROWS)
            hm = _mlp(hm, g_mlp_i, w1_b, w2_b, tm=META_ROWS, tf=cfg["tf"])
            w_in_i = cast[3]

    return h.reshape(b, seq, d)
```

```python
import functools

import jax
import jax.numpy as jnp
from jax import lax
from jax.experimental import pallas as pl
from jax.experimental.pallas import tpu as pltpu

F32 = jnp.float32
BF16 = jnp.bfloat16

EPS = 1e-6
HEAD_DIM = 128
CONV_K = 3
SUBLANES = 8
BF16_ROWS = 16
META_ROWS = 128
COL_CHUNK = 2 * HEAD_DIM
MLP_FIRST_SUB_ROWS = 256
VMEM_LIMIT = 60 * 1024 * 1024
NEG_BIG = -1e30
LOG2E = 1.4426950408889634
DEAD_LOG2 = -160.0


def _rms(x, g):
    ms = jnp.mean(x * x, axis=-1, keepdims=True)
    return x * lax.rsqrt(ms + EPS) * g


def _in_proj_kernel(x_ref, g_ref, w_ref, cw_ref, gco_ref, gqk_ref, ctx_ref, *rest,
                    tm, tiles_per_seq, n_riders):
    rider_in = rest[:n_riders]
    yc_ref, q_ref, k_ref, v_ref, tail_ref = rest[n_riders:n_riders + 5]
    rider_out = rest[n_riders + 5:2 * n_riders + 5]
    hn_even_ref, hn_odd_ref, cu_ref, carry_ref = rest[2 * n_riders + 5:]
    step = pl.program_id(0)
    cc = COL_CHUNK
    width = yc_ref.shape[1]
    n_chunks = width // cc

    def cast_riders():
        for src, dst in zip(rider_in, rider_out):
            dst[...] = src[...].astype(BF16)

    @pl.when(step == 0)
    def _():
        hn_even_ref[...] = _rms(x_ref[...], g_ref[...]).astype(BF16)
        cast_riders()

    @pl.when(jnp.logical_and(step > 0, ((step - 1) % tiles_per_seq) == 0))
    def _():
        carry_ref[...] = ctx_ref[...]

    def tile(hn_ref, hn_next_ref):
        def proj(section, c):
            col = section * width + c * cc
            return jnp.dot(hn_ref[...], w_ref[:, col:col + cc], preferred_element_type=F32)

        for c in range(n_chunks):
            cs = slice(c * cc, (c + 1) * cc)
            gate_b = proj(0, c)
            cu = proj(1, c) * proj(2, c)
            if c == 0:
                hn_next_ref[...] = _rms(x_ref[...], g_ref[...]).astype(BF16)
                cast_riders()
            cu_ref[c, SUBLANES:SUBLANES + tm, :] = cu
            cu_ref[c, 0:SUBLANES, :] = carry_ref[:, cs]
            tail = cu[tm - SUBLANES:tm, :]
            carry_ref[:, cs] = tail
            tail_ref[0, :, cs] = tail
            c1 = cu_ref[c, SUBLANES - 1:SUBLANES - 1 + tm, :]
            c2 = cu_ref[c, SUBLANES - 2:SUBLANES - 2 + tm, :]
            w = cw_ref[:, cs]
            y = gate_b * (w[0:1, :] * c2 + w[1:2, :] * c1 + w[2:3, :] * cu)
            for s in range(c * cc, (c + 1) * cc, HEAD_DIM):
                sl = slice(s, s + HEAD_DIM)
                yc_ref[:, sl] = _rms(y[:, s - c * cc:s - c * cc + HEAD_DIM],
                                     gco_ref[:, sl]).astype(BF16)

        gqk = gqk_ref[...]
        for section, o_ref, gain in ((3, q_ref, gqk[0:1, :]), (4, k_ref, gqk[1:2, :])):
            for c in range(n_chunks):
                acc = proj(section, c)
                for s in range(0, cc, HEAD_DIM):
                    o_ref[:, c * cc + s:c * cc + s + HEAD_DIM] = _rms(
                        acc[:, s:s + HEAD_DIM], gain).astype(BF16)
        for c in range(n_chunks):
            v_ref[:, c * cc:(c + 1) * cc] = proj(5, c).astype(BF16)

    @pl.when(jnp.logical_and(step > 0, step % 2 == 1))
    def _():
        tile(hn_even_ref, hn_odd_ref)

    @pl.when(jnp.logical_and(step > 0, step % 2 == 0))
    def _():
        tile(hn_odd_ref, hn_even_ref)


def _can_ride(a, n_tiles):
    return a.shape[1] % (n_tiles * BF16_ROWS) == 0


def _in_proj(h, g_mix, w, conv_w, g_co, g_qk, ctx_tail, riders=(), *, tm, tiles_per_seq):
    m, d = h.shape
    width = conv_w.shape[1]
    n_tiles = m // tm
    assert w.shape == (d, 6 * width) and width % COL_CHUNK == 0 and m % tm == 0
    assert all(_can_ride(a, n_tiles) for a, _ in riders)

    kern = functools.partial(_in_proj_kernel, tm=tm, tiles_per_seq=tiles_per_seq,
                             n_riders=len(riders))
    act = jax.ShapeDtypeStruct((m, width), BF16)
    row = pl.BlockSpec((tm, width), lambda s: (jnp.maximum(s - 1, 0), 0))
    const = lambda s: (0, 0)
    slab = lambda s: jnp.minimum(s, n_tiles - 1)
    return pl.pallas_call(
        kern,
        grid=(n_tiles + 1,),
        in_specs=[
            pl.BlockSpec((tm, d), lambda s: (slab(s), 0)),
            pl.BlockSpec((1, d), const),
            pl.BlockSpec(w.shape, const, pipeline_mode=pl.Buffered(1)),
            pl.BlockSpec((CONV_K, width), const),
            pl.BlockSpec((1, width), const),
            pl.BlockSpec((2, HEAD_DIM), const),
            pl.BlockSpec((SUBLANES, width), const),
        ] + [pl.BlockSpec((None, a.shape[1] // n_tiles, a.shape[2]),
                          functools.partial(lambda layer, s: (layer, slab(s), 0), layer))
             for a, layer in riders],
        out_specs=[row, row, row, row,
                   pl.BlockSpec((1, SUBLANES, width), lambda s: (jnp.maximum(s - 1, 0), 0, 0))]
        + [pl.BlockSpec((a.shape[1] // n_tiles, a.shape[2]), lambda s: (slab(s), 0))
           for a, _ in riders],
        out_shape=[act, act, act, act,
                   jax.ShapeDtypeStruct((n_tiles, SUBLANES, width), F32)]
        + [jax.ShapeDtypeStruct(a.shape[1:], BF16) for a, _ in riders],
        scratch_shapes=[
            pltpu.VMEM((tm, d), BF16),
            pltpu.VMEM((tm, d), BF16),
            pltpu.VMEM((width // COL_CHUNK, tm + SUBLANES, COL_CHUNK), F32),
            pltpu.VMEM((SUBLANES, width), F32),
        ],
        compiler_params=pltpu.CompilerParams(
            dimension_semantics=("arbitrary",),
            vmem_limit_bytes=VMEM_LIMIT),
        name="in_proj",
    )(h, g_mix, w, conv_w, g_co, g_qk, ctx_tail, *[a for a, _ in riders])


def _neg_abs(x):
    bits = lax.bitcast_convert_type(x, jnp.int32) | jnp.int32(-2 ** 31)
    return lax.bitcast_convert_type(bits, F32)


def _neg_suffix_matrix(n):
    rr = lax.broadcasted_iota(jnp.int32, (2 * n, n), 0)
    cc = lax.broadcasted_iota(jnp.int32, (2 * n, n), 1)
    rr = jnp.where(rr >= n, rr - n, rr)
    return jnp.where(rr >= cc, -1.0, 0.0).astype(BF16)


def _attn_kernel(*refs, seq, tq, own_pad, ctx_pad):
    has_ctx = ctx_pad is not None
    if has_ctx:
        (q_ref, k_ref, v_ref, kc_ref, vc_ref, g_ref, o_ref,
         uu_ref, uc_ref, acc_ref, r_ref) = refs
    else:
        q_ref, k_ref, v_ref, g_ref, o_ref, uu_ref, acc_ref, r_ref = refs
    tk = tq
    nq = seq // tq
    heads = q_ref.shape[2] // HEAD_DIM
    nt = (((1,), (1,)), ((), ()))

    uu_ref[...] = _neg_suffix_matrix(tk)
    if has_ctx:
        uc_ref[...] = _neg_suffix_matrix(kc_ref.shape[1])
    rr = lax.broadcasted_iota(jnp.int32, (tq, tk), 0)
    cc = lax.broadcasted_iota(jnp.int32, (tq, tk), 1)
    diag_mask = cc < rr
    if own_pad:
        diag_mask = jnp.logical_and(diag_mask, cc >= own_pad)

    def hcols(hh):
        return slice(hh * HEAD_DIM, (hh + 1) * HEAD_DIM)

    def sweep(qs, blocks, uu):
        hs = range(heads)
        zs = [[lax.dot_general(qs[hh], kbs[hh], nt, preferred_element_type=F32) for hh in hs]
              for kbs, _, _ in blocks]
        css = []
        for zb, (_, _, mask) in zip(zs, blocks):
            row = []
            for z in zb:
                nl = jnp.maximum(z, 0.0) + jnp.log2(1.0 + jnp.exp2(_neg_abs(z)))
                if mask is not None:
                    nl = jnp.where(mask, nl, 0.0)
                hi = nl.astype(BF16)
                lo = (nl - hi.astype(F32)).astype(BF16)
                row.append(jnp.dot(jnp.concatenate([hi, lo], axis=1), uu,
                                   preferred_element_type=F32))
            css.append(row)
        rs = [r_ref[hh] for hh in hs]
        avs = [None] * heads
        for zb, cb, (kbs, vbs, mask) in zip(zs, css, blocks):
            width = kbs[0].shape[0]
            for hh in hs:
                t = zb[hh] + cb[hh] + jnp.concatenate([rs[hh]] * (width // HEAD_DIM), axis=1)
                if mask is not None:
                    t = jnp.where(mask, t, NEG_BIG)
                a = jnp.exp2(t).astype(BF16)
                av = jnp.dot(a, vbs[hh], preferred_element_type=F32)
                avs[hh] = av if avs[hh] is None else avs[hh] + av
                rs[hh] = rs[hh] + jnp.broadcast_to(cb[hh][:, 0:1], (tq, HEAD_DIM))
        for hh in hs:
            r_ref[hh] = rs[hh]
            acc_ref[hh] += avs[hh]

    def kv_block(c0, mask):
        return ([k_ref[0, pl.ds(c0, tk), hcols(hh)] for hh in range(heads)],
                [v_ref[0, pl.ds(c0, tk), hcols(hh)] for hh in range(heads)], mask)

    def alive():
        return jnp.max(r_ref[...]) > DEAD_LOG2

    def q_block(qi):
        first = isinstance(qi, int) and qi == 0
        row0 = qi * tq if isinstance(qi, int) else pl.multiple_of(qi * tq, tq)
        qs = [q_ref[0, pl.ds(row0, tq), hcols(hh)] for hh in range(heads)]
        acc_ref[...] = jnp.zeros_like(acc_ref)
        r_ref[...] = jnp.zeros_like(r_ref)
        uu = uu_ref[...]

        if first:
            sweep(qs, [kv_block(row0, diag_mask)], uu)
            live = alive()
        else:
            sweep(qs, [kv_block(row0, diag_mask),
                       kv_block(pl.multiple_of(row0 - tk, tk), None)], uu)

            def kv_cond(carry):
                return jnp.logical_and(carry[0] < qi, carry[1])

            def kv_step(carry):
                n = carry[0]
                sweep(qs, [kv_block(pl.multiple_of((qi - 1 - n) * tk, tk), None)], uu)
                return n + 1, alive()

            _, live = lax.while_loop(kv_cond, kv_step, (jnp.int32(1), alive()))

        if has_ctx:
            @pl.when(live)
            def _():
                cw = kc_ref.shape[1]
                ctx_mask = lax.broadcasted_iota(jnp.int32, (tq, cw), 1) >= ctx_pad
                sweep(qs, [([kc_ref[0, :, hcols(hh)] for hh in range(heads)],
                            [vc_ref[0, :, hcols(hh)] for hh in range(heads)], ctx_mask)],
                      uc_ref[...])

        g = g_ref[...]
        for hh in range(heads):
            o_ref[0, pl.ds(row0, tq), hcols(hh)] = _rms(acc_ref[hh], g[:, hcols(hh)]).astype(BF16)

    q_block(0)
    if nq > 1:
        def q_step(qi, carry):
            q_block(qi)
            return carry
        lax.fori_loop(1, nq, q_step, 0)


def _attention(q, k, v, ctx, g_ao, *, tq, heads, own_pad, ctx_pad):
    b, seq, width = q.shape
    hw = heads * HEAD_DIM
    blk = pl.BlockSpec((1, seq, hw), lambda bi, hi: (bi, 0, hi))
    in_specs = [blk, blk, blk]
    args = [q, k, v]
    scratch = [pltpu.VMEM((2 * tq, tq), BF16)]
    if ctx is not None:
        kc, vc = ctx
        cblk = pl.BlockSpec((1, kc.shape[1], hw), lambda bi, hi: (0, 0, hi))
        in_specs += [cblk, cblk]
        args += [kc, vc]
        scratch.append(pltpu.VMEM((2 * kc.shape[1], kc.shape[1]), BF16))
    in_specs.append(pl.BlockSpec((1, hw), lambda bi, hi: (0, hi)))
    args.append(g_ao)
    kern = functools.partial(_attn_kernel, seq=seq, tq=tq, own_pad=own_pad,
                             ctx_pad=ctx_pad if ctx is not None else None)
    return pl.pallas_call(
        kern,
        grid=(b, width // hw),
        in_specs=in_specs,
        out_specs=blk,
        out_shape=jax.ShapeDtypeStruct((b, seq, width), BF16),
        scratch_shapes=scratch + [
            pltpu.VMEM((heads, tq, HEAD_DIM), F32),
            pltpu.VMEM((heads, tq, HEAD_DIM), F32),
        ],
        compiler_params=pltpu.CompilerParams(
            dimension_semantics=("arbitrary", "arbitrary"),
            vmem_limit_bytes=VMEM_LIMIT),
        name="sb_attention",
    )(*args)


def _out_proj_kernel(h_ref, yc_ref, ya_ref, w_ref, o_ref):
    half = yc_ref.shape[1]
    o_ref[...] = (h_ref[...]
                  + jnp.dot(yc_ref[...], w_ref[0:half, :], preferred_element_type=F32)
                  + jnp.dot(ya_ref[...], w_ref[half:, :], preferred_element_type=F32))


def _out_proj(h, yc, ya, w_out, *, tm):
    m, d = h.shape
    half = yc.shape[1]
    row = pl.BlockSpec((tm, d), lambda i: (i, 0))
    yrow = pl.BlockSpec((tm, half), lambda i: (i, 0))
    return pl.pallas_call(
        _out_proj_kernel,
        grid=(m // tm,),
        in_specs=[row, yrow, yrow, pl.BlockSpec(w_out.shape, lambda i: (0, 0))],
        out_specs=row,
        out_shape=jax.ShapeDtypeStruct((m, d), F32),
        compiler_params=pltpu.CompilerParams(
            dimension_semantics=("arbitrary",), vmem_limit_bytes=VMEM_LIMIT),
        name="out_proj",
    )(h, yc, ya, w_out)


def _mlp_kernel(h_ref, g_ref, w1_ref, w2_ref, o_ref, hn_ref, *, sub):
    f = pl.program_id(1)

    def ff_tile(hn):
        hid = jnp.dot(hn, w1_ref[...], preferred_element_type=F32)
        hid = jnp.square(jnp.maximum(hid, 0.0)).astype(BF16)
        return jnp.dot(hid, w2_ref[...], preferred_element_type=F32)

    @pl.when(f == 0)
    def _():
        for r in range(0, h_ref.shape[0], sub):
            rows = slice(r, r + sub)
            x = h_ref[rows, :]
            hn = _rms(x, g_ref[...]).astype(BF16)
            hn_ref[rows, :] = hn
            o_ref[rows, :] = x + ff_tile(hn)

    @pl.when(f > 0)
    def _():
        o_ref[...] += ff_tile(hn_ref[...])


def _mlp(h, g_mlp, w1, w2, *, tm, tf):
    m, d = h.shape
    d_ff = w1.shape[1]
    row = pl.BlockSpec((tm, d), lambda i, f: (i, 0))
    return pl.pallas_call(
        functools.partial(_mlp_kernel, sub=min(tm, MLP_FIRST_SUB_ROWS)),
        grid=(m // tm, d_ff // tf),
        in_specs=[row,
                  pl.BlockSpec((1, d), lambda i, f: (0, 0)),
                  pl.BlockSpec((d, tf), lambda i, f: (0, f)),
                  pl.BlockSpec((tf, d), lambda i, f: (f, 0))],
        out_specs=row,
        out_shape=jax.ShapeDtypeStruct((m, d), F32),
        scratch_shapes=[pltpu.VMEM((tm, d), BF16)],
        compiler_params=pltpu.CompilerParams(
            dimension_semantics=("arbitrary", "arbitrary"), vmem_limit_bytes=VMEM_LIMIT),
        name="mlp",
    )(h, g_mlp, w1, w2)


def _tiles(seq):
    return dict(tm_in=min(512, seq), tm_out=min(512, seq), tm_mlp=min(1024, seq), tf=1024,
                tq=min(256, seq), heads=4)


def kernel(x, meta_tokens, g_mix, w_in, conv_w, g_q, g_k, g_conv_out, g_attn_out, w_out,
           g_mlp, w_mlp_in, w_mlp_out):
    b, seq, d = x.shape
    depth = w_in.shape[0]
    n_meta = meta_tokens.shape[0]
    width = conv_w.shape[2]
    assert n_meta <= META_ROWS and seq % META_ROWS == 0
    cfg = _tiles(seq)
    meta_pad = META_ROWS - n_meta

    h = x.reshape(b * seq, d)
    hm = jnp.pad(meta_tokens.astype(x.dtype), ((meta_pad, 0), (0, 0)))
    zero_tail = jnp.zeros((SUBLANES, width), F32)
    later_w = (w_mlp_in, w_mlp_out, w_out)
    ride = all(_can_ride(a, b * seq // cfg["tm_in"]) for a in later_w + (w_in,))
    w_in_i = w_in[0].astype(BF16)

    for i in range(depth):
        g_mix_i = g_mix[i].reshape(1, d)
        g_co_i = g_conv_out[i].reshape(1, width)
        g_ao_i = g_attn_out[i].reshape(1, width)
        g_mlp_i = g_mlp[i].reshape(1, d)
        g_qk_i = jnp.stack([g_q[i] * (HEAD_DIM ** -0.5 * LOG2E), g_k[i]])
        last = i + 1 == depth
        riders = []
        if ride:
            riders = [(a, i) for a in later_w] + ([] if last else [(w_in, i + 1)])

        ycm, qm, km, vm, tail_m = _in_proj(hm, g_mix_i, w_in_i, conv_w[i], g_co_i, g_qk_i,
                                           zero_tail, tm=META_ROWS, tiles_per_seq=1)
        yc, q, k, v, _, *cast = _in_proj(h, g_mix_i, w_in_i, conv_w[i], g_co_i, g_qk_i,
                                         tail_m[0], riders,
                                         tm=cfg["tm_in"], tiles_per_seq=seq // cfg["tm_in"])
        if not ride:
            cast = [a[i].astype(BF16) for a in later_w] + ([] if last else [w_in[i + 1].astype(BF16)])
        w1_b, w2_b, w_out_b = cast[:3]
        ya = _attention(q.reshape(b, seq, width), k.reshape(b, seq, width),
                        v.reshape(b, seq, width), (km[None], vm[None]), g_ao_i,
                        tq=cfg["tq"], heads=cfg["heads"], own_pad=0, ctx_pad=meta_pad)
        h = _out_proj(h, yc, ya.reshape(b * seq, width), w_out_b, tm=cfg["tm_out"])
        h = _mlp(h, g_mlp_i, w1_b, w2_b, tm=cfg["tm_mlp"], tf=cfg["tf"])

        if not last:
            yam = _attention(qm[None], km[None], vm[None], None, g_ao_i,
                             tq=META_ROWS, heads=cfg["heads"], own_pad=meta_pad, ctx_pad=None)
            hm = _out_proj(hm, ycm, yam[0], w_out_b, tm=META_ROWS)
            hm = _mlp(hm, g_mlp_i, w1_b, w2_b, tm=META_ROWS, tf=cfg["tf"])
            w_in_i = cast[3]

    return h.reshape(b, seq, d)
```

```python
import functools

import jax
import jax.numpy as jnp
from jax import lax
from jax.experimental import pallas as pl
from jax.experimental.pallas import tpu as pltpu

F32 = jnp.float32
BF16 = jnp.bfloat16

EPS = 1e-6
HEAD_DIM = 128
CONV_K = 3
SUBLANES = 8
BF16_ROWS = 16
META_ROWS = 128
COL_CHUNK = 2 * HEAD_DIM
MLP_FIRST_SUB_ROWS = 256
VMEM_LIMIT = 60 * 1024 * 1024
NEG_BIG = -1e30
LOG2E = 1.4426950408889634
DEAD_LOG2 = -160.0


def _rms(x, g):
    ms = jnp.mean(x * x, axis=-1, keepdims=True)
    return x * lax.rsqrt(ms + EPS) * g


def _in_proj_kernel(x_ref, g_ref, w_ref, cw_ref, gco_ref, gqk_ref, ctx_ref, *rest,
                    tm, tiles_per_seq, n_riders):
    rider_in = rest[:n_riders]
    yc_ref, q_ref, k_ref, v_ref, tail_ref = rest[n_riders:n_riders + 5]
    rider_out = rest[n_riders + 5:2 * n_riders + 5]
    hn_ref, cu_ref, carry_ref = rest[2 * n_riders + 5:]
    step = pl.program_id(0)
    cc = COL_CHUNK
    width = yc_ref.shape[1]
    n_chunks = width // cc

    def cast_riders():
        for src, dst in zip(rider_in, rider_out):
            dst[...] = src[...].astype(BF16)

    @pl.when(step == 0)
    def _():
        hn_ref[0] = _rms(x_ref[...], g_ref[...]).astype(BF16)
        cast_riders()

    @pl.when(jnp.logical_and(step > 0, ((step - 1) % tiles_per_seq) == 0))
    def _():
        carry_ref[...] = ctx_ref[...]

    @pl.when(step > 0)
    def _tile():
        slot = (step - 1) % 2

        def proj(section, c):
            col = section * width + c * cc
            return jnp.dot(hn_ref[slot], w_ref[:, col:col + cc], preferred_element_type=F32)

        for c in range(n_chunks):
            cs = slice(c * cc, (c + 1) * cc)
            gate_b = proj(0, c)
            cu = proj(1, c) * proj(2, c)
            if c == 0:
                hn_ref[1 - slot] = _rms(x_ref[...], g_ref[...]).astype(BF16)
                cast_riders()
            cu_ref[c, SUBLANES:SUBLANES + tm, :] = cu
            cu_ref[c, 0:SUBLANES, :] = carry_ref[:, cs]
            tail = cu[tm - SUBLANES:tm, :]
            carry_ref[:, cs] = tail
            tail_ref[0, :, cs] = tail
            c1 = cu_ref[c, SUBLANES - 1:SUBLANES - 1 + tm, :]
            c2 = cu_ref[c, SUBLANES - 2:SUBLANES - 2 + tm, :]
            w = cw_ref[:, cs]
            y = gate_b * (w[0:1, :] * c2 + w[1:2, :] * c1 + w[2:3, :] * cu)
            for s in range(c * cc, (c + 1) * cc, HEAD_DIM):
                sl = slice(s, s + HEAD_DIM)
                yc_ref[:, sl] = _rms(y[:, s - c * cc:s - c * cc + HEAD_DIM],
                                     gco_ref[:, sl]).astype(BF16)

        gqk = gqk_ref[...]
        for section, o_ref, gain in ((3, q_ref, gqk[0:1, :]), (4, k_ref, gqk[1:2, :])):
            for c in range(n_chunks):
                acc = proj(section, c)
                for s in range(0, cc, HEAD_DIM):
                    o_ref[:, c * cc + s:c * cc + s + HEAD_DIM] = _rms(
                        acc[:, s:s + HEAD_DIM], gain).astype(BF16)
        for c in range(n_chunks):
            v_ref[:, c * cc:(c + 1) * cc] = proj(5, c).astype(BF16)


def _can_ride(a, n_tiles):
    return a.shape[1] % (n_tiles * BF16_ROWS) == 0


def _in_proj(h, g_mix, w, conv_w, g_co, g_qk, ctx_tail, riders=(), *, tm, tiles_per_seq):
    m, d = h.shape
    width = conv_w.shape[1]
    n_tiles = m // tm
    assert w.shape == (d, 6 * width) and width % COL_CHUNK == 0 and m % tm == 0
    assert all(_can_ride(a, n_tiles) for a, _ in riders)

    kern = functools.partial(_in_proj_kernel, tm=tm, tiles_per_seq=tiles_per_seq,
                             n_riders=len(riders))
    act = jax.ShapeDtypeStruct((m, width), BF16)
    row = pl.BlockSpec((tm, width), lambda s: (jnp.maximum(s - 1, 0), 0))
    const = lambda s: (0, 0)
    slab = lambda s: jnp.minimum(s, n_tiles - 1)
    return pl.pallas_call(
        kern,
        grid=(n_tiles + 1,),
        in_specs=[
            pl.BlockSpec((tm, d), lambda s: (slab(s), 0)),
            pl.BlockSpec((1, d), const),
            pl.BlockSpec(w.shape, const, pipeline_mode=pl.Buffered(1)),
            pl.BlockSpec((CONV_K, width), const),
            pl.BlockSpec((1, width), const),
            pl.BlockSpec((2, HEAD_DIM), const),
            pl.BlockSpec((SUBLANES, width), const),
        ] + [pl.BlockSpec((None, a.shape[1] // n_tiles, a.shape[2]),
                          functools.partial(lambda layer, s: (layer, slab(s), 0), layer))
             for a, layer in riders],
        out_specs=[row, row, row, row,
                   pl.BlockSpec((1, SUBLANES, width), lambda s: (jnp.maximum(s - 1, 0), 0, 0))]
        + [pl.BlockSpec((a.shape[1] // n_tiles, a.shape[2]), lambda s: (slab(s), 0))
           for a, _ in riders],
        out_shape=[act, act, act, act,
                   jax.ShapeDtypeStruct((n_tiles, SUBLANES, width), F32)]
        + [jax.ShapeDtypeStruct(a.shape[1:], BF16) for a, _ in riders],
        scratch_shapes=[
            pltpu.VMEM((2, tm, d), BF16),
            pltpu.VMEM((width // COL_CHUNK, tm + SUBLANES, COL_CHUNK), F32),
            pltpu.VMEM((SUBLANES, width), F32),
        ],
        compiler_params=pltpu.CompilerParams(
            dimension_semantics=("arbitrary",),
            vmem_limit_bytes=VMEM_LIMIT),
        name="in_proj",
    )(h, g_mix, w, conv_w, g_co, g_qk, ctx_tail, *[a for a, _ in riders])


def _neg_abs(x):
    bits = lax.bitcast_convert_type(x, jnp.int32) | jnp.int32(-2 ** 31)
    return lax.bitcast_convert_type(bits, F32)


def _neg_suffix_matrix(n):
    rr = lax.broadcasted_iota(jnp.int32, (n, n), 0)
    cc = lax.broadcasted_iota(jnp.int32, (n, n), 1)
    return jnp.where(rr >= cc, -1.0, 0.0).astype(BF16)


def _attn_kernel(*refs, seq, tq, own_pad, ctx_pad):
    has_ctx = ctx_pad is not None
    if has_ctx:
        (q_ref, k_ref, v_ref, kc_ref, vc_ref, g_ref, o_ref,
         uu_ref, uc_ref, acc_ref, r_ref) = refs
    else:
        q_ref, k_ref, v_ref, g_ref, o_ref, uu_ref, acc_ref, r_ref = refs
    tk = tq
    nq = seq // tq
    heads = q_ref.shape[2] // HEAD_DIM
    nt = (((1,), (1,)), ((), ()))

    uu_ref[...] = _neg_suffix_matrix(tk)
    if has_ctx:
        uc_ref[...] = _neg_suffix_matrix(kc_ref.shape[1])
    rr = lax.broadcasted_iota(jnp.int32, (tq, tk), 0)
    cc = lax.broadcasted_iota(jnp.int32, (tq, tk), 1)
    diag_mask = cc < rr
    if own_pad:
        diag_mask = jnp.logical_and(diag_mask, cc >= own_pad)

    def hcols(hh):
        return slice(hh * HEAD_DIM, (hh + 1) * HEAD_DIM)

    def sweep(qs, blocks, uu):
        hs = range(heads)
        zs = [[lax.dot_general(qs[hh], kbs[hh], nt, preferred_element_type=F32) for hh in hs]
              for kbs, _, _ in blocks]
        css = []
        for zb, (_, _, mask) in zip(zs, blocks):
            row = []
            for z in zb:
                nl = jnp.maximum(z, 0.0) + jnp.log2(1.0 + jnp.exp2(_neg_abs(z)))
                if mask is not None:
                    nl = jnp.where(mask, nl, 0.0)
                row.append(jnp.dot(nl.astype(BF16), uu, preferred_element_type=F32))
            css.append(row)
        rs = [r_ref[hh] for hh in hs]
        avs = [None] * heads
        for zb, cb, (kbs, vbs, mask) in zip(zs, css, blocks):
            width = kbs[0].shape[0]
            for hh in hs:
                t = zb[hh] + cb[hh] + jnp.concatenate([rs[hh]] * (width // HEAD_DIM), axis=1)
                if mask is not None:
                    t = jnp.where(mask, t, NEG_BIG)
                a = jnp.exp2(t).astype(BF16)
                av = jnp.dot(a, vbs[hh], preferred_element_type=F32)
                avs[hh] = av if avs[hh] is None else avs[hh] + av
                rs[hh] = rs[hh] + jnp.broadcast_to(cb[hh][:, 0:1], (tq, HEAD_DIM))
        for hh in hs:
            r_ref[hh] = rs[hh]
            acc_ref[hh] += avs[hh]

    def kv_block(c0, mask):
        return ([k_ref[0, pl.ds(c0, tk), hcols(hh)] for hh in range(heads)],
                [v_ref[0, pl.ds(c0, tk), hcols(hh)] for hh in range(heads)], mask)

    def alive():
        return jnp.max(r_ref[...]) > DEAD_LOG2

    def q_block(qi):
        first = isinstance(qi, int) and qi == 0
        row0 = qi * tq if isinstance(qi, int) else pl.multiple_of(qi * tq, tq)
        qs = [q_ref[0, pl.ds(row0, tq), hcols(hh)] for hh in range(heads)]
        acc_ref[...] = jnp.zeros_like(acc_ref)
        r_ref[...] = jnp.zeros_like(r_ref)
        uu = uu_ref[...]

        if first:
            sweep(qs, [kv_block(row0, diag_mask)], uu)
            live = alive()
        else:
            sweep(qs, [kv_block(row0, diag_mask),
                       kv_block(pl.multiple_of(row0 - tk, tk), None)], uu)

            def kv_cond(carry):
                return jnp.logical_and(carry[0] < qi, carry[1])

            def kv_step(carry):
                n = carry[0]
                sweep(qs, [kv_block(pl.multiple_of((qi - 1 - n) * tk, tk), None)], uu)
                return n + 1, alive()

            _, live = lax.while_loop(kv_cond, kv_step, (jnp.int32(1), alive()))

        if has_ctx:
            @pl.when(live)
            def _():
                cw = kc_ref.shape[1]
                ctx_mask = lax.broadcasted_iota(jnp.int32, (tq, cw), 1) >= ctx_pad
                sweep(qs, [([kc_ref[0, :, hcols(hh)] for hh in range(heads)],
                            [vc_ref[0, :, hcols(hh)] for hh in range(heads)], ctx_mask)],
                      uc_ref[...])

        g = g_ref[...]
        for hh in range(heads):
            o_ref[0, pl.ds(row0, tq), hcols(hh)] = _rms(acc_ref[hh], g[:, hcols(hh)]).astype(BF16)

    q_block(0)
    if nq > 1:
        def q_step(qi, carry):
            q_block(qi)
            return carry
        lax.fori_loop(1, nq, q_step, 0)


def _attention(q, k, v, ctx, g_ao, *, tq, heads, own_pad, ctx_pad):
    b, seq, width = q.shape
    hw = heads * HEAD_DIM
    blk = pl.BlockSpec((1, seq, hw), lambda bi, hi: (bi, 0, hi))
    in_specs = [blk, blk, blk]
    args = [q, k, v]
    scratch = [pltpu.VMEM((tq, tq), BF16)]
    if ctx is not None:
        kc, vc = ctx
        cblk = pl.BlockSpec((1, kc.shape[1], hw), lambda bi, hi: (0, 0, hi))
        in_specs += [cblk, cblk]
        args += [kc, vc]
        scratch.append(pltpu.VMEM((kc.shape[1], kc.shape[1]), BF16))
    in_specs.append(pl.BlockSpec((1, hw), lambda bi, hi: (0, hi)))
    args.append(g_ao)
    kern = functools.partial(_attn_kernel, seq=seq, tq=tq, own_pad=own_pad,
                             ctx_pad=ctx_pad if ctx is not None else None)
    return pl.pallas_call(
        kern,
        grid=(b, width // hw),
        in_specs=in_specs,
        out_specs=blk,
        out_shape=jax.ShapeDtypeStruct((b, seq, width), BF16),
        scratch_shapes=scratch + [
            pltpu.VMEM((heads, tq, HEAD_DIM), F32),
            pltpu.VMEM((heads, tq, HEAD_DIM), F32),
        ],
        compiler_params=pltpu.CompilerParams(
            dimension_semantics=("arbitrary", "arbitrary"),
            vmem_limit_bytes=VMEM_LIMIT),
        name="sb_attention",
    )(*args)


def _out_proj_kernel(h_ref, yc_ref, ya_ref, w_ref, o_ref):
    half = yc_ref.shape[1]
    o_ref[...] = (h_ref[...]
                  + jnp.dot(yc_ref[...], w_ref[0:half, :], preferred_element_type=F32)
                  + jnp.dot(ya_ref[...], w_ref[half:, :], preferred_element_type=F32))


def _out_proj(h, yc, ya, w_out, *, tm):
    m, d = h.shape
    half = yc.shape[1]
    row = pl.BlockSpec((tm, d), lambda i: (i, 0))
    yrow = pl.BlockSpec((tm, half), lambda i: (i, 0))
    return pl.pallas_call(
        _out_proj_kernel,
        grid=(m // tm,),
        in_specs=[row, yrow, yrow, pl.BlockSpec(w_out.shape, lambda i: (0, 0))],
        out_specs=row,
        out_shape=jax.ShapeDtypeStruct((m, d), F32),
        compiler_params=pltpu.CompilerParams(
            dimension_semantics=("arbitrary",), vmem_limit_bytes=VMEM_LIMIT),
        name="out_proj",
    )(h, yc, ya, w_out)


def _mlp_kernel(h_ref, g_ref, w1_ref, w2_ref, o_ref, hn_ref, *, sub):
    f = pl.program_id(1)

    def ff_tile(hn):
        hid = jnp.dot(hn, w1_ref[...], preferred_element_type=F32)
        hid = jnp.square(jnp.maximum(hid, 0.0)).astype(BF16)
        return jnp.dot(hid, w2_ref[...], preferred_element_type=F32)

    @pl.when(f == 0)
    def _():
        for r in range(0, h_ref.shape[0], sub):
            rows = slice(r, r + sub)
            x = h_ref[rows, :]
            hn = _rms(x, g_ref[...]).astype(BF16)
            hn_ref[rows, :] = hn
            o_ref[rows, :] = x + ff_tile(hn)

    @pl.when(f > 0)
    def _():
        o_ref[...] += ff_tile(hn_ref[...])


def _mlp(h, g_mlp, w1, w2, *, tm, tf):
    m, d = h.shape
    d_ff = w1.shape[1]
    row = pl.BlockSpec((tm, d), lambda i, f: (i, 0))
    return pl.pallas_call(
        functools.partial(_mlp_kernel, sub=min(tm, MLP_FIRST_SUB_ROWS)),
        grid=(m // tm, d_ff // tf),
        in_specs=[row,
                  pl.BlockSpec((1, d), lambda i, f: (0, 0)),
                  pl.BlockSpec((d, tf), lambda i, f: (0, f)),
                  pl.BlockSpec((tf, d), lambda i, f: (f, 0))],
        out_specs=row,
        out_shape=jax.ShapeDtypeStruct((m, d), F32),
        scratch_shapes=[pltpu.VMEM((tm, d), BF16)],
        compiler_params=pltpu.CompilerParams(
            dimension_semantics=("arbitrary", "arbitrary"), vmem_limit_bytes=VMEM_LIMIT),
        name="mlp",
    )(h, g_mlp, w1, w2)


def _tiles(seq):
    return dict(tm_in=min(512, seq), tm_out=min(512, seq), tm_mlp=min(1024, seq), tf=1024,
                tq=min(256, seq), heads=4)


def kernel(x, meta_tokens, g_mix, w_in, conv_w, g_q, g_k, g_conv_out, g_attn_out, w_out,
           g_mlp, w_mlp_in, w_mlp_out):
    b, seq, d = x.shape
    depth = w_in.shape[0]
    n_meta = meta_tokens.shape[0]
    width = conv_w.shape[2]
    assert n_meta <= META_ROWS and seq % META_ROWS == 0
    cfg = _tiles(seq)
    meta_pad = META_ROWS - n_meta

    h = x.reshape(b * seq, d)
    hm = jnp.pad(meta_tokens.astype(x.dtype), ((meta_pad, 0), (0, 0)))
    zero_tail = jnp.zeros((SUBLANES, width), F32)
    later_w = (w_mlp_in, w_mlp_out, w_out)
    ride = all(_can_ride(a, b * seq // cfg["tm_in"]) for a in later_w + (w_in,))
    w_in_i = w_in[0].astype(BF16)

    for i in range(depth):
        g_mix_i = g_mix[i].reshape(1, d)
        g_co_i = g_conv_out[i].reshape(1, width)
        g_ao_i = g_attn_out[i].reshape(1, width)
        g_mlp_i = g_mlp[i].reshape(1, d)
        g_qk_i = jnp.stack([g_q[i] * (HEAD_DIM ** -0.5 * LOG2E), g_k[i]])
        last = i + 1 == depth
        riders = []
        if ride:
            riders = [(a, i) for a in later_w] + ([] if last else [(w_in, i + 1)])

        ycm, qm, km, vm, tail_m = _in_proj(hm, g_mix_i, w_in_i, conv_w[i], g_co_i, g_qk_i,
                                           zero_tail, tm=META_ROWS, tiles_per_seq=1)
        yc, q, k, v, _, *cast = _in_proj(h, g_mix_i, w_in_i, conv_w[i], g_co_i, g_qk_i,
                                         tail_m[0], riders,
                                         tm=cfg["tm_in"], tiles_per_seq=seq // cfg["tm_in"])
        if not ride:
            cast = [a[i].astype(BF16) for a in later_w] + ([] if last else [w_in[i + 1].astype(BF16)])
        w1_b, w2_b, w_out_b = cast[:3]
        ya = _attention(q.reshape(b, seq, width), k.reshape(b, seq, width),
                        v.reshape(b, seq, width), (km[None], vm[None]), g_ao_i,
                        tq=cfg["tq"], heads=cfg["heads"], own_pad=0, ctx_pad=meta_pad)
        h = _out_proj(h, yc, ya.reshape(b * seq, width), w_out_b, tm=cfg["tm_out"])
        h = _mlp(h, g_mlp_i, w1_b, w2_b, tm=cfg["tm_mlp"], tf=cfg["tf"])

        if not last:
            yam = _attention(qm[None], km[None], vm[None], None, g_ao_i,
                             tq=META_ROWS, heads=cfg["heads"], own_pad=meta_pad, ctx_pad=None)
            hm = _out_proj(hm, ycm, yam[0], w_out_b, tm=META_ROWS)
            hm = _mlp(hm, g_mlp_i, w1_b, w2_b, tm=META_ROWS, tf=cfg["tf"])
            w_in_i = cast[3]

    return h.reshape(b, seq, d)
```

```python
import functools

import jax
import jax.numpy as jnp
from jax import lax
from jax.experimental import pallas as pl
from jax.experimental.pallas import tpu as pltpu

F32 = jnp.float32
BF16 = jnp.bfloat16

EPS = 1e-6
HEAD_DIM = 128
CONV_K = 3
SUBLANES = 8
BF16_ROWS = 16
META_ROWS = 128
COL_CHUNK = 2 * HEAD_DIM
MLP_FIRST_SUB_ROWS = 256
VMEM_LIMIT = 60 * 1024 * 1024
NEG_BIG = -1e30
LOG2E = 1.4426950408889634
DEAD_LOG2 = -160.0


def _rms(x, g):
    ms = jnp.mean(x * x, axis=-1, keepdims=True)
    return x * lax.rsqrt(ms + EPS) * g


def _in_proj_kernel(x_ref, g_ref, w_ref, cw_ref, gco_ref, gqk_ref, ctx_ref, *rest,
                    tm, tiles_per_seq, n_riders):
    rider_in = rest[:n_riders]
    yc_ref, q_ref, k_ref, v_ref, tail_ref = rest[n_riders:n_riders + 5]
    rider_out = rest[n_riders + 5:2 * n_riders + 5]
    hn_ref, carry_ref = rest[2 * n_riders + 5:]
    step = pl.program_id(0)
    cc = COL_CHUNK
    width = yc_ref.shape[1]
    n_chunks = width // cc

    def cast_riders():
        for src, dst in zip(rider_in, rider_out):
            dst[...] = src[...].astype(BF16)

    @pl.when(step == 0)
    def _():
        hn_ref[0] = _rms(x_ref[...], g_ref[...]).astype(BF16)
        cast_riders()

    @pl.when(jnp.logical_and(step > 0, ((step - 1) % tiles_per_seq) == 0))
    def _():
        carry_ref[...] = ctx_ref[...]

    @pl.when(step > 0)
    def _tile():
        slot = (step - 1) % 2

        def proj(section, c):
            col = section * width + c * cc
            return jnp.dot(hn_ref[slot], w_ref[:, col:col + cc], preferred_element_type=F32)

        for c in range(n_chunks):
            cs = slice(c * cc, (c + 1) * cc)
            gate_b = proj(0, c)
            cu = proj(1, c) * proj(2, c)
            if c == 0:
                hn_ref[1 - slot] = _rms(x_ref[...], g_ref[...]).astype(BF16)
                cast_riders()
            prev = carry_ref[:, cs]
            tail = cu[tm - SUBLANES:tm, :]
            carry_ref[:, cs] = tail
            tail_ref[0, :, cs] = tail
            row = lax.broadcasted_iota(jnp.int32, (tm, cc), 0)
            hist1 = jnp.broadcast_to(prev[SUBLANES - 1:SUBLANES, :], (tm, cc))
            hist2 = jnp.broadcast_to(prev[SUBLANES - 2:SUBLANES - 1, :], (tm, cc))
            c1 = jnp.where(row >= 1, pltpu.roll(cu, 1, axis=0), hist1)
            c2 = jnp.where(row >= 2, pltpu.roll(cu, 2, axis=0),
                           jnp.where(row == 1, hist1, hist2))
            w = cw_ref[:, cs]
            y = gate_b * (w[0:1, :] * c2 + w[1:2, :] * c1 + w[2:3, :] * cu)
            for s in range(c * cc, (c + 1) * cc, HEAD_DIM):
                sl = slice(s, s + HEAD_DIM)
                yc_ref[:, sl] = _rms(y[:, s - c * cc:s - c * cc + HEAD_DIM],
                                     gco_ref[:, sl]).astype(BF16)

        gqk = gqk_ref[...]
        for section, o_ref, gain in ((3, q_ref, gqk[0:1, :]), (4, k_ref, gqk[1:2, :])):
            for c in range(n_chunks):
                acc = proj(section, c)
                for s in range(0, cc, HEAD_DIM):
                    o_ref[:, c * cc + s:c * cc + s + HEAD_DIM] = _rms(
                        acc[:, s:s + HEAD_DIM], gain).astype(BF16)
        for c in range(n_chunks):
            v_ref[:, c * cc:(c + 1) * cc] = proj(5, c).astype(BF16)


def _can_ride(a, n_tiles):
    return a.shape[1] % (n_tiles * BF16_ROWS) == 0


def _in_proj(h, g_mix, w, conv_w, g_co, g_qk, ctx_tail, riders=(), *, tm, tiles_per_seq):
    m, d = h.shape
    width = conv_w.shape[1]
    n_tiles = m // tm
    assert w.shape == (d, 6 * width) and width % COL_CHUNK == 0 and m % tm == 0
    assert all(_can_ride(a, n_tiles) for a, _ in riders)

    kern = functools.partial(_in_proj_kernel, tm=tm, tiles_per_seq=tiles_per_seq,
                             n_riders=len(riders))
    act = jax.ShapeDtypeStruct((m, width), BF16)
    row = pl.BlockSpec((tm, width), lambda s: (jnp.maximum(s - 1, 0), 0))
    const = lambda s: (0, 0)
    slab = lambda s: jnp.minimum(s, n_tiles - 1)
    return pl.pallas_call(
        kern,
        grid=(n_tiles + 1,),
        in_specs=[
            pl.BlockSpec((tm, d), lambda s: (slab(s), 0)),
            pl.BlockSpec((1, d), const),
            pl.BlockSpec(w.shape, const, pipeline_mode=pl.Buffered(1)),
            pl.BlockSpec((CONV_K, width), const),
            pl.BlockSpec((1, width), const),
            pl.BlockSpec((2, HEAD_DIM), const),
            pl.BlockSpec((SUBLANES, width), const),
        ] + [pl.BlockSpec((None, a.shape[1] // n_tiles, a.shape[2]),
                          functools.partial(lambda layer, s: (layer, slab(s), 0), layer))
             for a, layer in riders],
        out_specs=[row, row, row, row,
                   pl.BlockSpec((1, SUBLANES, width), lambda s: (jnp.maximum(s - 1, 0), 0, 0))]
        + [pl.BlockSpec((a.shape[1] // n_tiles, a.shape[2]), lambda s: (slab(s), 0))
           for a, _ in riders],
        out_shape=[act, act, act, act,
                   jax.ShapeDtypeStruct((n_tiles, SUBLANES, width), F32)]
        + [jax.ShapeDtypeStruct(a.shape[1:], BF16) for a, _ in riders],
        scratch_shapes=[
            pltpu.VMEM((2, tm, d), BF16),
            pltpu.VMEM((SUBLANES, width), F32),
        ],
        compiler_params=pltpu.CompilerParams(
            dimension_semantics=("arbitrary",),
            vmem_limit_bytes=VMEM_LIMIT),
        name="in_proj",
    )(h, g_mix, w, conv_w, g_co, g_qk, ctx_tail, *[a for a, _ in riders])


def _neg_abs(x):
    bits = lax.bitcast_convert_type(x, jnp.int32) | jnp.int32(-2 ** 31)
    return lax.bitcast_convert_type(bits, F32)


def _neg_suffix_matrix(n):
    rr = lax.broadcasted_iota(jnp.int32, (n, n), 0)
    cc = lax.broadcasted_iota(jnp.int32, (n, n), 1)
    return jnp.where(rr >= cc, -1.0, 0.0).astype(BF16)


def _attn_kernel(*refs, seq, tq, own_pad, ctx_pad):
    has_ctx = ctx_pad is not None
    if has_ctx:
        (q_ref, k_ref, v_ref, kc_ref, vc_ref, g_ref, o_ref,
         uu_ref, uc_ref, acc_ref, r_ref) = refs
    else:
        q_ref, k_ref, v_ref, g_ref, o_ref, uu_ref, acc_ref, r_ref = refs
    tk = tq
    nq = seq // tq
    heads = q_ref.shape[2] // HEAD_DIM
    nt = (((1,), (1,)), ((), ()))

    uu_ref[...] = _neg_suffix_matrix(tk)
    if has_ctx:
        uc_ref[...] = _neg_suffix_matrix(kc_ref.shape[1])
    rr = lax.broadcasted_iota(jnp.int32, (tq, tk), 0)
    cc = lax.broadcasted_iota(jnp.int32, (tq, tk), 1)
    diag_mask = cc < rr
    if own_pad:
        diag_mask = jnp.logical_and(diag_mask, cc >= own_pad)

    def hcols(hh):
        return slice(hh * HEAD_DIM, (hh + 1) * HEAD_DIM)

    def sweep(qs, blocks, uu):
        hs = range(heads)
        zs = [[lax.dot_general(qs[hh], kbs[hh], nt, preferred_element_type=F32) for hh in hs]
              for kbs, _, _ in blocks]
        css = []
        for zb, (_, _, mask) in zip(zs, blocks):
            row = []
            for z in zb:
                nl = jnp.maximum(z, 0.0) + jnp.log2(1.0 + jnp.exp2(_neg_abs(z)))
                if mask is not None:
                    nl = jnp.where(mask, nl, 0.0)
                row.append(jnp.dot(nl.astype(BF16), uu, preferred_element_type=F32))
            css.append(row)
        rs = [r_ref[hh] for hh in hs]
        avs = [None] * heads
        for zb, cb, (kbs, vbs, mask) in zip(zs, css, blocks):
            width = kbs[0].shape[0]
            for hh in hs:
                t = zb[hh] + cb[hh] + jnp.concatenate([rs[hh]] * (width // HEAD_DIM), axis=1)
                if mask is not None:
                    t = jnp.where(mask, t, NEG_BIG)
                a = jnp.exp2(t).astype(BF16)
                av = jnp.dot(a, vbs[hh], preferred_element_type=F32)
                avs[hh] = av if avs[hh] is None else avs[hh] + av
                rs[hh] = rs[hh] + jnp.broadcast_to(cb[hh][:, 0:1], (tq, HEAD_DIM))
        for hh in hs:
            r_ref[hh] = rs[hh]
            acc_ref[hh] += avs[hh]

    def kv_block(c0, mask):
        return ([k_ref[0, pl.ds(c0, tk), hcols(hh)] for hh in range(heads)],
                [v_ref[0, pl.ds(c0, tk), hcols(hh)] for hh in range(heads)], mask)

    def alive():
        return jnp.max(r_ref[...]) > DEAD_LOG2

    def q_block(qi):
        first = isinstance(qi, int) and qi == 0
        row0 = qi * tq if isinstance(qi, int) else pl.multiple_of(qi * tq, tq)
        qs = [q_ref[0, pl.ds(row0, tq), hcols(hh)] for hh in range(heads)]
        acc_ref[...] = jnp.zeros_like(acc_ref)
        r_ref[...] = jnp.zeros_like(r_ref)
        uu = uu_ref[...]

        if first:
            sweep(qs, [kv_block(row0, diag_mask)], uu)
            live = alive()
        else:
            sweep(qs, [kv_block(row0, diag_mask),
                       kv_block(pl.multiple_of(row0 - tk, tk), None)], uu)

            def kv_cond(carry):
                return jnp.logical_and(carry[0] < qi, carry[1])

            def kv_step(carry):
                n = carry[0]
                sweep(qs, [kv_block(pl.multiple_of((qi - 1 - n) * tk, tk), None)], uu)
                return n + 1, alive()

            _, live = lax.while_loop(kv_cond, kv_step, (jnp.int32(1), alive()))

        if has_ctx:
            @pl.when(live)
            def _():
                cw = kc_ref.shape[1]
                ctx_mask = lax.broadcasted_iota(jnp.int32, (tq, cw), 1) >= ctx_pad
                sweep(qs, [([kc_ref[0, :, hcols(hh)] for hh in range(heads)],
                            [vc_ref[0, :, hcols(hh)] for hh in range(heads)], ctx_mask)],
                      uc_ref[...])

        g = g_ref[...]
        for hh in range(heads):
            o_ref[0, pl.ds(row0, tq), hcols(hh)] = _rms(acc_ref[hh], g[:, hcols(hh)]).astype(BF16)

    q_block(0)
    if nq > 1:
        def q_step(qi, carry):
            q_block(qi)
            return carry
        lax.fori_loop(1, nq, q_step, 0)


def _attention(q, k, v, ctx, g_ao, *, tq, heads, own_pad, ctx_pad):
    b, seq, width = q.shape
    hw = heads * HEAD_DIM
    blk = pl.BlockSpec((1, seq, hw), lambda bi, hi: (bi, 0, hi))
    in_specs = [blk, blk, blk]
    args = [q, k, v]
    scratch = [pltpu.VMEM((tq, tq), BF16)]
    if ctx is not None:
        kc, vc = ctx
        cblk = pl.BlockSpec((1, kc.shape[1], hw), lambda bi, hi: (0, 0, hi))
        in_specs += [cblk, cblk]
        args += [kc, vc]
        scratch.append(pltpu.VMEM((kc.shape[1], kc.shape[1]), BF16))
    in_specs.append(pl.BlockSpec((1, hw), lambda bi, hi: (0, hi)))
    args.append(g_ao)
    kern = functools.partial(_attn_kernel, seq=seq, tq=tq, own_pad=own_pad,
                             ctx_pad=ctx_pad if ctx is not None else None)
    return pl.pallas_call(
        kern,
        grid=(b, width // hw),
        in_specs=in_specs,
        out_specs=blk,
        out_shape=jax.ShapeDtypeStruct((b, seq, width), BF16),
        scratch_shapes=scratch + [
            pltpu.VMEM((heads, tq, HEAD_DIM), F32),
            pltpu.VMEM((heads, tq, HEAD_DIM), F32),
        ],
        compiler_params=pltpu.CompilerParams(
            dimension_semantics=("arbitrary", "arbitrary"),
            vmem_limit_bytes=VMEM_LIMIT),
        name="sb_attention",
    )(*args)


def _out_proj_kernel(h_ref, yc_ref, ya_ref, w_ref, o_ref):
    half = yc_ref.shape[1]
    o_ref[...] = (h_ref[...]
                  + jnp.dot(yc_ref[...], w_ref[0:half, :], preferred_element_type=F32)
                  + jnp.dot(ya_ref[...], w_ref[half:, :], preferred_element_type=F32))


def _out_proj(h, yc, ya, w_out, *, tm):
    m, d = h.shape
    half = yc.shape[1]
    row = pl.BlockSpec((tm, d), lambda i: (i, 0))
    yrow = pl.BlockSpec((tm, half), lambda i: (i, 0))
    return pl.pallas_call(
        _out_proj_kernel,
        grid=(m // tm,),
        in_specs=[row, yrow, yrow, pl.BlockSpec(w_out.shape, lambda i: (0, 0))],
        out_specs=row,
        out_shape=jax.ShapeDtypeStruct((m, d), F32),
        compiler_params=pltpu.CompilerParams(
            dimension_semantics=("arbitrary",), vmem_limit_bytes=VMEM_LIMIT),
        name="out_proj",
    )(h, yc, ya, w_out)


def _mlp_kernel(h_ref, g_ref, w1_ref, w2_ref, o_ref, hn_ref, *, sub):
    f = pl.program_id(1)

    def ff_tile(hn):
        hid = jnp.dot(hn, w1_ref[...], preferred_element_type=F32)
        hid = jnp.square(jnp.maximum(hid, 0.0)).astype(BF16)
        return jnp.dot(hid, w2_ref[...], preferred_element_type=F32)

    @pl.when(f == 0)
    def _():
        for r in range(0, h_ref.shape[0], sub):
            rows = slice(r, r + sub)
            x = h_ref[rows, :]
            hn = _rms(x, g_ref[...]).astype(BF16)
            hn_ref[rows, :] = hn
            o_ref[rows, :] = x + ff_tile(hn)

    @pl.when(f > 0)
    def _():
        o_ref[...] += ff_tile(hn_ref[...])


def _mlp(h, g_mlp, w1, w2, *, tm, tf):
    m, d = h.shape
    d_ff = w1.shape[1]
    row = pl.BlockSpec((tm, d), lambda i, f: (i, 0))
    return pl.pallas_call(
        functools.partial(_mlp_kernel, sub=min(tm, MLP_FIRST_SUB_ROWS)),
        grid=(m // tm, d_ff // tf),
        in_specs=[row,
                  pl.BlockSpec((1, d), lambda i, f: (0, 0)),
                  pl.BlockSpec((d, tf), lambda i, f: (0, f)),
                  pl.BlockSpec((tf, d), lambda i, f: (f, 0))],
        out_specs=row,
        out_shape=jax.ShapeDtypeStruct((m, d), F32),
        scratch_shapes=[pltpu.VMEM((tm, d), BF16)],
        compiler_params=pltpu.CompilerParams(
            dimension_semantics=("arbitrary", "arbitrary"), vmem_limit_bytes=VMEM_LIMIT),
        name="mlp",
    )(h, g_mlp, w1, w2)


def _tiles(seq):
    return dict(tm_in=min(512, seq), tm_out=min(512, seq), tm_mlp=min(1024, seq), tf=1024,
                tq=min(256, seq), heads=4)


def kernel(x, meta_tokens, g_mix, w_in, conv_w, g_q, g_k, g_conv_out, g_attn_out, w_out,
           g_mlp, w_mlp_in, w_mlp_out):
    b, seq, d = x.shape
    depth = w_in.shape[0]
    n_meta = meta_tokens.shape[0]
    width = conv_w.shape[2]
    assert n_meta <= META_ROWS and seq % META_ROWS == 0
    cfg = _tiles(seq)
    meta_pad = META_ROWS - n_meta

    h = x.reshape(b * seq, d)
    hm = jnp.pad(meta_tokens.astype(x.dtype), ((meta_pad, 0), (0, 0)))
    zero_tail = jnp.zeros((SUBLANES, width), F32)
    later_w = (w_mlp_in, w_mlp_out, w_out)
    ride = all(_can_ride(a, b * seq // cfg["tm_in"]) for a in later_w + (w_in,))
    w_in_i = w_in[0].astype(BF16)

    for i in range(depth):
        g_mix_i = g_mix[i].reshape(1, d)
        g_co_i = g_conv_out[i].reshape(1, width)
        g_ao_i = g_attn_out[i].reshape(1, width)
        g_mlp_i = g_mlp[i].reshape(1, d)
        g_qk_i = jnp.stack([g_q[i] * (HEAD_DIM ** -0.5 * LOG2E), g_k[i]])
        last = i + 1 == depth
        riders = []
        if ride:
            riders = [(a, i) for a in later_w] + ([] if last else [(w_in, i + 1)])

        ycm, qm, km, vm, tail_m = _in_proj(hm, g_mix_i, w_in_i, conv_w[i], g_co_i, g_qk_i,
                                           zero_tail, tm=META_ROWS, tiles_per_seq=1)
        yc, q, k, v, _, *cast = _in_proj(h, g_mix_i, w_in_i, conv_w[i], g_co_i, g_qk_i,
                                         tail_m[0], riders,
                                         tm=cfg["tm_in"], tiles_per_seq=seq // cfg["tm_in"])
        if not ride:
            cast = [a[i].astype(BF16) for a in later_w] + ([] if last else [w_in[i + 1].astype(BF16)])
        w1_b, w2_b, w_out_b = cast[:3]
        ya = _attention(q.reshape(b, seq, width), k.reshape(b, seq, width),
                        v.reshape(b, seq, width), (km[None], vm[None]), g_ao_i,
                        tq=cfg["tq"], heads=cfg["heads"], own_pad=0, ctx_pad=meta_pad)
        h = _out_proj(h, yc, ya.reshape(b * seq, width), w_out_b, tm=cfg["tm_out"])
        h = _mlp(h, g_mlp_i, w1_b, w2_b, tm=cfg["tm_mlp"], tf=cfg["tf"])

        if not last:
            yam = _attention(qm[None], km[None], vm[None], None, g_ao_i,
                             tq=META_ROWS, heads=cfg["heads"], own_pad=meta_pad, ctx_pad=None)
            hm = _out_proj(hm, ycm, yam[0], w_out_b, tm=META_ROWS)
            hm = _mlp(hm, g_mlp_i, w1_b, w2_b, tm=META_ROWS, tf=cfg["tf"])
            w_in_i = cast[3]

    return h.reshape(b, seq, d)
```

```python
import functools

import jax
import jax.numpy as jnp
from jax import lax
from jax.experimental import pallas as pl
from jax.experimental.pallas import tpu as pltpu

F32 = jnp.float32
BF16 = jnp.bfloat16

EPS = 1e-6
HEAD_DIM = 128
CONV_K = 3
SUBLANES = 8
BF16_ROWS = 16
META_ROWS = 128
COL_CHUNK = 2 * HEAD_DIM
MLP_FIRST_SUB_ROWS = 256
VMEM_LIMIT = 60 * 1024 * 1024
NEG_BIG = -1e30
LOG2E = 1.4426950408889634
DEAD_LOG2 = -160.0


def _rms(x, g):
    ms = jnp.mean(x * x, axis=-1, keepdims=True)
    return x * lax.rsqrt(ms + EPS) * g


def _in_proj_kernel(x_ref, g_ref, w_ref, cw_ref, gco_ref, gqk_ref, ctx_ref, *rest,
                    tm, tiles_per_seq, n_riders):
    rider_in = rest[:n_riders]
    yc_ref, q_ref, k_ref, v_ref, tail_ref = rest[n_riders:n_riders + 5]
    rider_out = rest[n_riders + 5:2 * n_riders + 5]
    hn_ref, carry_ref = rest[2 * n_riders + 5:]
    step = pl.program_id(0)
    cc = COL_CHUNK
    width = yc_ref.shape[1]
    n_chunks = width // cc

    def cast_riders():
        for src, dst in zip(rider_in, rider_out):
            dst[...] = src[...].astype(BF16)

    @pl.when(step == 0)
    def _():
        hn_ref[0] = _rms(x_ref[...], g_ref[...]).astype(BF16)
        cast_riders()

    @pl.when(jnp.logical_and(step > 0, ((step - 1) % tiles_per_seq) == 0))
    def _():
        carry_ref[...] = ctx_ref[...]

    @pl.when(step > 0)
    def _tile():
        slot = (step - 1) % 2

        def proj(section, c):
            col = section * width + c * cc
            return jnp.dot(hn_ref[slot], w_ref[:, col:col + cc], preferred_element_type=F32)

        for c in range(n_chunks):
            cs = slice(c * cc, (c + 1) * cc)
            gate_b = proj(0, c)
            cu = proj(1, c) * proj(2, c)
            if c == 0:
                hn_ref[1 - slot] = _rms(x_ref[...], g_ref[...]).astype(BF16)
            prev = carry_ref[:, cs]
            tail = cu[tm - SUBLANES:tm, :]
            carry_ref[:, cs] = tail
            tail_ref[0, :, cs] = tail
            row = lax.broadcasted_iota(jnp.int32, (tm, cc), 0)
            hist1 = jnp.broadcast_to(prev[SUBLANES - 1:SUBLANES, :], (tm, cc))
            hist2 = jnp.broadcast_to(prev[SUBLANES - 2:SUBLANES - 1, :], (tm, cc))
            c1 = jnp.where(row >= 1, pltpu.roll(cu, 1, axis=0), hist1)
            c2 = jnp.where(row >= 2, pltpu.roll(cu, 2, axis=0),
                           jnp.where(row == 1, hist1, hist2))
            w = cw_ref[:, cs]
            y = gate_b * (w[0:1, :] * c2 + w[1:2, :] * c1 + w[2:3, :] * cu)
            for s in range(c * cc, (c + 1) * cc, HEAD_DIM):
                sl = slice(s, s + HEAD_DIM)
                yc_ref[:, sl] = _rms(y[:, s - c * cc:s - c * cc + HEAD_DIM],
                                     gco_ref[:, sl]).astype(BF16)

        gqk = gqk_ref[...]
        for section, o_ref, gain in ((3, q_ref, gqk[0:1, :]), (4, k_ref, gqk[1:2, :])):
            for c in range(n_chunks):
                acc = proj(section, c)
                if section == 3 and c == 1:
                    cast_riders()
                for s in range(0, cc, HEAD_DIM):
                    o_ref[:, c * cc + s:c * cc + s + HEAD_DIM] = _rms(
                        acc[:, s:s + HEAD_DIM], gain).astype(BF16)
        for c in range(n_chunks):
            v_ref[:, c * cc:(c + 1) * cc] = proj(5, c).astype(BF16)


def _can_ride(a, n_tiles):
    return a.shape[1] % (n_tiles * BF16_ROWS) == 0


def _in_proj(h, g_mix, w, conv_w, g_co, g_qk, ctx_tail, riders=(), *, tm, tiles_per_seq):
    m, d = h.shape
    width = conv_w.shape[1]
    n_tiles = m // tm
    assert w.shape == (d, 6 * width) and width % COL_CHUNK == 0 and m % tm == 0
    assert all(_can_ride(a, n_tiles) for a, _ in riders)

    kern = functools.partial(_in_proj_kernel, tm=tm, tiles_per_seq=tiles_per_seq,
                             n_riders=len(riders))
    act = jax.ShapeDtypeStruct((m, width), BF16)
    row = pl.BlockSpec((tm, width), lambda s: (jnp.maximum(s - 1, 0), 0))
    const = lambda s: (0, 0)
    slab = lambda s: jnp.minimum(s, n_tiles - 1)
    return pl.pallas_call(
        kern,
        grid=(n_tiles + 1,),
        in_specs=[
            pl.BlockSpec((tm, d), lambda s: (slab(s), 0)),
            pl.BlockSpec((1, d), const),
            pl.BlockSpec(w.shape, const, pipeline_mode=pl.Buffered(1)),
            pl.BlockSpec((CONV_K, width), const),
            pl.BlockSpec((1, width), const),
            pl.BlockSpec((2, HEAD_DIM), const),
            pl.BlockSpec((SUBLANES, width), const),
        ] + [pl.BlockSpec((None, a.shape[1] // n_tiles, a.shape[2]),
                          functools.partial(lambda layer, s: (layer, slab(s), 0), layer))
             for a, layer in riders],
        out_specs=[row, row, row, row,
                   pl.BlockSpec((1, SUBLANES, width), lambda s: (jnp.maximum(s - 1, 0), 0, 0))]
        + [pl.BlockSpec((a.shape[1] // n_tiles, a.shape[2]), lambda s: (slab(s), 0))
           for a, _ in riders],
        out_shape=[act, act, act, act,
                   jax.ShapeDtypeStruct((n_tiles, SUBLANES, width), F32)]
        + [jax.ShapeDtypeStruct(a.shape[1:], BF16) for a, _ in riders],
        scratch_shapes=[
            pltpu.VMEM((2, tm, d), BF16),
            pltpu.VMEM((SUBLANES, width), F32),
        ],
        compiler_params=pltpu.CompilerParams(
            dimension_semantics=("arbitrary",),
            vmem_limit_bytes=VMEM_LIMIT),
        name="in_proj",
    )(h, g_mix, w, conv_w, g_co, g_qk, ctx_tail, *[a for a, _ in riders])


def _neg_abs(x):
    bits = lax.bitcast_convert_type(x, jnp.int32) | jnp.int32(-2 ** 31)
    return lax.bitcast_convert_type(bits, F32)


def _neg_suffix_matrix(n):
    rr = lax.broadcasted_iota(jnp.int32, (n, n), 0)
    cc = lax.broadcasted_iota(jnp.int32, (n, n), 1)
    return jnp.where(rr >= cc, -1.0, 0.0).astype(BF16)


def _attn_kernel(*refs, seq, tq, own_pad, ctx_pad):
    has_ctx = ctx_pad is not None
    if has_ctx:
        (q_ref, k_ref, v_ref, kc_ref, vc_ref, g_ref, o_ref,
         uu_ref, uc_ref, acc_ref, r_ref) = refs
    else:
        q_ref, k_ref, v_ref, g_ref, o_ref, uu_ref, acc_ref, r_ref = refs
    tk = tq
    nq = seq // tq
    heads = q_ref.shape[2] // HEAD_DIM
    nt = (((1,), (1,)), ((), ()))

    uu_ref[...] = _neg_suffix_matrix(tk)
    if has_ctx:
        uc_ref[...] = _neg_suffix_matrix(kc_ref.shape[1])
    rr = lax.broadcasted_iota(jnp.int32, (tq, tk), 0)
    cc = lax.broadcasted_iota(jnp.int32, (tq, tk), 1)
    diag_mask = cc < rr
    if own_pad:
        diag_mask = jnp.logical_and(diag_mask, cc >= own_pad)

    def hcols(hh):
        return slice(hh * HEAD_DIM, (hh + 1) * HEAD_DIM)

    def sweep(qs, blocks, uu):
        hs = range(heads)
        zs = [[lax.dot_general(qs[hh], kbs[hh], nt, preferred_element_type=F32) for hh in hs]
              for kbs, _, _ in blocks]
        css = []
        for zb, (_, _, mask) in zip(zs, blocks):
            row = []
            for z in zb:
                nl = jnp.maximum(z, 0.0) + jnp.log2(1.0 + jnp.exp2(_neg_abs(z)))
                if mask is not None:
                    nl = jnp.where(mask, nl, 0.0)
                row.append(jnp.dot(nl.astype(BF16), uu, preferred_element_type=F32))
            css.append(row)
        rs = [r_ref[hh] for hh in hs]
        avs = [None] * heads
        for zb, cb, (kbs, vbs, mask) in zip(zs, css, blocks):
            width = kbs[0].shape[0]
            for hh in hs:
                t = zb[hh] + cb[hh] + jnp.concatenate([rs[hh]] * (width // HEAD_DIM), axis=1)
                if mask is not None:
                    t = jnp.where(mask, t, NEG_BIG)
                a = jnp.exp2(t).astype(BF16)
                av = jnp.dot(a, vbs[hh], preferred_element_type=F32)
                avs[hh] = av if avs[hh] is None else avs[hh] + av
                rs[hh] = rs[hh] + jnp.broadcast_to(cb[hh][:, 0:1], (tq, HEAD_DIM))
        for hh in hs:
            r_ref[hh] = rs[hh]
            acc_ref[hh] += avs[hh]

    def kv_block(c0, mask):
        return ([k_ref[0, pl.ds(c0, tk), hcols(hh)] for hh in range(heads)],
                [v_ref[0, pl.ds(c0, tk), hcols(hh)] for hh in range(heads)], mask)

    def alive():
        return jnp.max(r_ref[...]) > DEAD_LOG2

    def q_block(qi):
        first = isinstance(qi, int) and qi == 0
        row0 = qi * tq if isinstance(qi, int) else pl.multiple_of(qi * tq, tq)
        qs = [q_ref[0, pl.ds(row0, tq), hcols(hh)] for hh in range(heads)]
        acc_ref[...] = jnp.zeros_like(acc_ref)
        r_ref[...] = jnp.zeros_like(r_ref)
        uu = uu_ref[...]

        if first:
            sweep(qs, [kv_block(row0, diag_mask)], uu)
            live = alive()
        else:
            sweep(qs, [kv_block(row0, diag_mask),
                       kv_block(pl.multiple_of(row0 - tk, tk), None)], uu)

            def kv_cond(carry):
                return jnp.logical_and(carry[0] < qi, carry[1])

            def kv_step(carry):
                n = carry[0]
                sweep(qs, [kv_block(pl.multiple_of((qi - 1 - n) * tk, tk), None)], uu)
                return n + 1, alive()

            _, live = lax.while_loop(kv_cond, kv_step, (jnp.int32(1), alive()))

        if has_ctx:
            @pl.when(live)
            def _():
                cw = kc_ref.shape[1]
                ctx_mask = lax.broadcasted_iota(jnp.int32, (tq, cw), 1) >= ctx_pad
                sweep(qs, [([kc_ref[0, :, hcols(hh)] for hh in range(heads)],
                            [vc_ref[0, :, hcols(hh)] for hh in range(heads)], ctx_mask)],
                      uc_ref[...])

        g = g_ref[...]
        for hh in range(heads):
            o_ref[0, pl.ds(row0, tq), hcols(hh)] = _rms(acc_ref[hh], g[:, hcols(hh)]).astype(BF16)

    q_block(0)
    if nq > 1:
        def q_step(qi, carry):
            q_block(qi)
            return carry
        lax.fori_loop(1, nq, q_step, 0)


def _attention(q, k, v, ctx, g_ao, *, tq, heads, own_pad, ctx_pad):
    b, seq, width = q.shape
    hw = heads * HEAD_DIM
    blk = pl.BlockSpec((1, seq, hw), lambda bi, hi: (bi, 0, hi))
    in_specs = [blk, blk, blk]
    args = [q, k, v]
    scratch = [pltpu.VMEM((tq, tq), BF16)]
    if ctx is not None:
        kc, vc = ctx
        cblk = pl.BlockSpec((1, kc.shape[1], hw), lambda bi, hi: (0, 0, hi))
        in_specs += [cblk, cblk]
        args += [kc, vc]
        scratch.append(pltpu.VMEM((kc.shape[1], kc.shape[1]), BF16))
    in_specs.append(pl.BlockSpec((1, hw), lambda bi, hi: (0, hi)))
    args.append(g_ao)
    kern = functools.partial(_attn_kernel, seq=seq, tq=tq, own_pad=own_pad,
                             ctx_pad=ctx_pad if ctx is not None else None)
    return pl.pallas_call(
        kern,
        grid=(b, width // hw),
        in_specs=in_specs,
        out_specs=blk,
        out_shape=jax.ShapeDtypeStruct((b, seq, width), BF16),
        scratch_shapes=scratch + [
            pltpu.VMEM((heads, tq, HEAD_DIM), F32),
            pltpu.VMEM((heads, tq, HEAD_DIM), F32),
        ],
        compiler_params=pltpu.CompilerParams(
            dimension_semantics=("arbitrary", "arbitrary"),
            vmem_limit_bytes=VMEM_LIMIT),
        name="sb_attention",
    )(*args)


def _out_proj_kernel(h_ref, yc_ref, ya_ref, w_ref, o_ref):
    half = yc_ref.shape[1]
    o_ref[...] = (h_ref[...]
                  + jnp.dot(yc_ref[...], w_ref[0:half, :], preferred_element_type=F32)
                  + jnp.dot(ya_ref[...], w_ref[half:, :], preferred_element_type=F32))


def _out_proj(h, yc, ya, w_out, *, tm):
    m, d = h.shape
    half = yc.shape[1]
    row = pl.BlockSpec((tm, d), lambda i: (i, 0))
    yrow = pl.BlockSpec((tm, half), lambda i: (i, 0))
    return pl.pallas_call(
        _out_proj_kernel,
        grid=(m // tm,),
        in_specs=[row, yrow, yrow, pl.BlockSpec(w_out.shape, lambda i: (0, 0))],
        out_specs=row,
        out_shape=jax.ShapeDtypeStruct((m, d), F32),
        compiler_params=pltpu.CompilerParams(
            dimension_semantics=("arbitrary",), vmem_limit_bytes=VMEM_LIMIT),
        name="out_proj",
    )(h, yc, ya, w_out)


def _mlp_kernel(h_ref, g_ref, w1_ref, w2_ref, o_ref, hn_ref, *, sub):
    f = pl.program_id(1)

    def ff_tile(hn):
        hid = jnp.dot(hn, w1_ref[...], preferred_element_type=F32)
        hid = jnp.square(jnp.maximum(hid, 0.0)).astype(BF16)
        return jnp.dot(hid, w2_ref[...], preferred_element_type=F32)

    @pl.when(f == 0)
    def _():
        for r in range(0, h_ref.shape[0], sub):
            rows = slice(r, r + sub)
            x = h_ref[rows, :]
            hn = _rms(x, g_ref[...]).astype(BF16)
            hn_ref[rows, :] = hn
            o_ref[rows, :] = x + ff_tile(hn)

    @pl.when(f > 0)
    def _():
        o_ref[...] += ff_tile(hn_ref[...])


def _mlp(h, g_mlp, w1, w2, *, tm, tf):
    m, d = h.shape
    d_ff = w1.shape[1]
    row = pl.BlockSpec((tm, d), lambda i, f: (i, 0))
    return pl.pallas_call(
        functools.partial(_mlp_kernel, sub=min(tm, MLP_FIRST_SUB_ROWS)),
        grid=(m // tm, d_ff // tf),
        in_specs=[row,
                  pl.BlockSpec((1, d), lambda i, f: (0, 0)),
                  pl.BlockSpec((d, tf), lambda i, f: (0, f)),
                  pl.BlockSpec((tf, d), lambda i, f: (f, 0))],
        out_specs=row,
        out_shape=jax.ShapeDtypeStruct((m, d), F32),
        scratch_shapes=[pltpu.VMEM((tm, d), BF16)],
        compiler_params=pltpu.CompilerParams(
            dimension_semantics=("arbitrary", "arbitrary"), vmem_limit_bytes=VMEM_LIMIT),
        name="mlp",
    )(h, g_mlp, w1, w2)


def _tiles(seq):
    return dict(tm_in=min(512, seq), tm_out=min(512, seq), tm_mlp=min(1024, seq), tf=1024,
                tq=min(256, seq), heads=4)


def kernel(x, meta_tokens, g_mix, w_in, conv_w, g_q, g_k, g_conv_out, g_attn_out, w_out,
           g_mlp, w_mlp_in, w_mlp_out):
    b, seq, d = x.shape
    depth = w_in.shape[0]
    n_meta = meta_tokens.shape[0]
    width = conv_w.shape[2]
    assert n_meta <= META_ROWS and seq % META_ROWS == 0
    cfg = _tiles(seq)
    meta_pad = META_ROWS - n_meta

    h = x.reshape(b * seq, d)
    hm = jnp.pad(meta_tokens.astype(x.dtype), ((meta_pad, 0), (0, 0)))
    zero_tail = jnp.zeros((SUBLANES, width), F32)
    later_w = (w_mlp_in, w_mlp_out, w_out)
    ride = all(_can_ride(a, b * seq // cfg["tm_in"]) for a in later_w + (w_in,))
    w_in_i = w_in[0].astype(BF16)

    for i in range(depth):
        g_mix_i = g_mix[i].reshape(1, d)
        g_co_i = g_conv_out[i].reshape(1, width)
        g_ao_i = g_attn_out[i].reshape(1, width)
        g_mlp_i = g_mlp[i].reshape(1, d)
        g_qk_i = jnp.stack([g_q[i] * (HEAD_DIM ** -0.5 * LOG2E), g_k[i]])
        last = i + 1 == depth
        riders = []
        if ride:
            riders = [(a, i) for a in later_w] + ([] if last else [(w_in, i + 1)])

        ycm, qm, km, vm, tail_m = _in_proj(hm, g_mix_i, w_in_i, conv_w[i], g_co_i, g_qk_i,
                                           zero_tail, tm=META_ROWS, tiles_per_seq=1)
        yc, q, k, v, _, *cast = _in_proj(h, g_mix_i, w_in_i, conv_w[i], g_co_i, g_qk_i,
                                         tail_m[0], riders,
                                         tm=cfg["tm_in"], tiles_per_seq=seq // cfg["tm_in"])
        if not ride:
            cast = [a[i].astype(BF16) for a in later_w] + ([] if last else [w_in[i + 1].astype(BF16)])
        w1_b, w2_b, w_out_b = cast[:3]
        ya = _attention(q.reshape(b, seq, width), k.reshape(b, seq, width),
                        v.reshape(b, seq, width), (km[None], vm[None]), g_ao_i,
                        tq=cfg["tq"], heads=cfg["heads"], own_pad=0, ctx_pad=meta_pad)
        h = _out_proj(h, yc, ya.reshape(b * seq, width), w_out_b, tm=cfg["tm_out"])
        h = _mlp(h, g_mlp_i, w1_b, w2_b, tm=cfg["tm_mlp"], tf=cfg["tf"])

        if not last:
            yam = _attention(qm[None], km[None], vm[None], None, g_ao_i,
                             tq=META_ROWS, heads=cfg["heads"], own_pad=meta_pad, ctx_pad=None)
            hm = _out_proj(hm, ycm, yam[0], w_out_b, tm=META_ROWS)
            hm = _mlp(hm, g_mlp_i, w1_b, w2_b, tm=META_ROWS, tf=cfg["tf"])
            w_in_i = cast[3]

    return h.reshape(b, seq, d)
```

```python
import functools

import jax
import jax.numpy as jnp
from jax import lax
from jax.experimental import pallas as pl
from jax.experimental.pallas import tpu as pltpu

F32 = jnp.float32
BF16 = jnp.bfloat16

EPS = 1e-6
HEAD_DIM = 128
CONV_K = 3
SUBLANES = 8
BF16_ROWS = 16
META_ROWS = 128
COL_CHUNK = 2 * HEAD_DIM
MLP_FIRST_SUB_ROWS = 256
VMEM_LIMIT = 60 * 1024 * 1024
NEG_BIG = -1e30
LOG2E = 1.4426950408889634
DEAD_LOG2 = -160.0


def _rms(x, g):
    ms = jnp.mean(x * x, axis=-1, keepdims=True)
    return x * lax.rsqrt(ms + EPS) * g


def _in_proj_kernel(x_ref, g_ref, w_ref, cw_ref, gco_ref, gqk_ref, ctx_ref, *rest,
                    tm, tiles_per_seq, n_riders):
    rider_in = rest[:n_riders]
    yc_ref, q_ref, k_ref, v_ref, tail_ref = rest[n_riders:n_riders + 5]
    rider_out = rest[n_riders + 5:2 * n_riders + 5]
    hn_ref, carry_ref = rest[2 * n_riders + 5:]
    step = pl.program_id(0)
    cc = COL_CHUNK
    width = yc_ref.shape[1]
    n_chunks = width // cc

    def cast_riders():
        for src, dst in zip(rider_in, rider_out):
            dst[...] = src[...].astype(BF16)

    @pl.when(step == 0)
    def _():
        hn_ref[0] = _rms(x_ref[...], g_ref[...]).astype(BF16)
        cast_riders()

    @pl.when(jnp.logical_and(step > 0, ((step - 1) % tiles_per_seq) == 0))
    def _():
        carry_ref[...] = ctx_ref[...]

    @pl.when(step > 0)
    def _tile():
        slot = (step - 1) % 2

        def proj(section, c):
            col = section * width + c * cc
            return jnp.dot(hn_ref[slot], w_ref[:, col:col + cc], preferred_element_type=F32)

        for c in range(n_chunks):
            cs = slice(c * cc, (c + 1) * cc)
            gate_b = proj(0, c)
            cu = proj(1, c) * proj(2, c)
            if c == 0:
                hn_ref[1 - slot] = _rms(x_ref[...], g_ref[...]).astype(BF16)
                cast_riders()
            prev = carry_ref[:, cs]
            tail = cu[tm - SUBLANES:tm, :]
            carry_ref[:, cs] = tail
            tail_ref[0, :, cs] = tail
            row = lax.broadcasted_iota(jnp.int32, (tm, cc), 0)
            hist1 = jnp.broadcast_to(prev[SUBLANES - 1:SUBLANES, :], (tm, cc))
            hist2 = jnp.broadcast_to(prev[SUBLANES - 2:SUBLANES - 1, :], (tm, cc))
            c1 = jnp.where(row >= 1, pltpu.roll(cu, 1, axis=0), hist1)
            c2 = jnp.where(row >= 2, pltpu.roll(cu, 2, axis=0),
                           jnp.where(row == 1, hist1, hist2))
            w = cw_ref[:, cs]
            y = gate_b * (w[0:1, :] * c2 + w[1:2, :] * c1 + w[2:3, :] * cu)
            for s in range(c * cc, (c + 1) * cc, HEAD_DIM):
                sl = slice(s, s + HEAD_DIM)
                yc_ref[:, sl] = _rms(y[:, s - c * cc:s - c * cc + HEAD_DIM],
                                     gco_ref[:, sl]).astype(BF16)

        gqk = gqk_ref[...]
        for section, o_ref, gain in ((3, q_ref, gqk[0:1, :]), (4, k_ref, gqk[1:2, :])):
            for c in range(n_chunks):
                acc = proj(section, c)
                for s in range(0, cc, HEAD_DIM):
                    o_ref[:, c * cc + s:c * cc + s + HEAD_DIM] = _rms(
                        acc[:, s:s + HEAD_DIM], gain).astype(BF16)
        for c in range(n_chunks):
            v_ref[:, c * cc:(c + 1) * cc] = proj(5, c).astype(BF16)


def _can_ride(a, n_tiles):
    return a.shape[1] % (n_tiles * BF16_ROWS) == 0


def _in_proj(h, g_mix, w, conv_w, g_co, g_qk, ctx_tail, riders=(), *, tm, tiles_per_seq):
    m, d = h.shape
    width = conv_w.shape[1]
    n_tiles = m // tm
    assert w.shape == (d, 6 * width) and width % COL_CHUNK == 0 and m % tm == 0
    assert all(_can_ride(a, n_tiles) for a, _ in riders)

    kern = functools.partial(_in_proj_kernel, tm=tm, tiles_per_seq=tiles_per_seq,
                             n_riders=len(riders))
    act = jax.ShapeDtypeStruct((m, width), BF16)
    row = pl.BlockSpec((tm, width), lambda s: (jnp.maximum(s - 1, 0), 0))
    const = lambda s: (0, 0)
    slab = lambda s: jnp.minimum(s, n_tiles - 1)
    return pl.pallas_call(
        kern,
        grid=(n_tiles + 1,),
        in_specs=[
            pl.BlockSpec((tm, d), lambda s: (slab(s), 0)),
            pl.BlockSpec((1, d), const),
            pl.BlockSpec(w.shape, const, pipeline_mode=pl.Buffered(1)),
            pl.BlockSpec((CONV_K, width), const),
            pl.BlockSpec((1, width), const),
            pl.BlockSpec((2, HEAD_DIM), const),
            pl.BlockSpec((SUBLANES, width), const),
        ] + [pl.BlockSpec((None, a.shape[1] // n_tiles, a.shape[2]),
                          functools.partial(lambda layer, s: (layer, slab(s), 0), layer))
             for a, layer in riders],
        out_specs=[row, row, row, row,
                   pl.BlockSpec((1, SUBLANES, width), lambda s: (jnp.maximum(s - 1, 0), 0, 0))]
        + [pl.BlockSpec((a.shape[1] // n_tiles, a.shape[2]), lambda s: (slab(s), 0))
           for a, _ in riders],
        out_shape=[act, act, act, act,
                   jax.ShapeDtypeStruct((n_tiles, SUBLANES, width), F32)]
        + [jax.ShapeDtypeStruct(a.shape[1:], BF16) for a, _ in riders],
        scratch_shapes=[
            pltpu.VMEM((2, tm, d), BF16),
            pltpu.VMEM((SUBLANES, width), F32),
        ],
        compiler_params=pltpu.CompilerParams(
            dimension_semantics=("arbitrary",),
            vmem_limit_bytes=VMEM_LIMIT),
        name="in_proj",
    )(h, g_mix, w, conv_w, g_co, g_qk, ctx_tail, *[a for a, _ in riders])


def _neg_abs(x):
    bits = lax.bitcast_convert_type(x, jnp.int32) | jnp.int32(-2 ** 31)
    return lax.bitcast_convert_type(bits, F32)


def _neg_suffix_matrix(n):
    rr = lax.broadcasted_iota(jnp.int32, (n, n), 0)
    cc = lax.broadcasted_iota(jnp.int32, (n, n), 1)
    return jnp.where(rr >= cc, -1.0, 0.0).astype(BF16)


def _attn_kernel(*refs, seq, tq, own_pad, ctx_pad):
    has_ctx = ctx_pad is not None
    if has_ctx:
        (q_ref, k_ref, v_ref, kc_ref, vc_ref, g_ref, o_ref,
         uu_ref, uc_ref, acc_ref, r_ref) = refs
    else:
        q_ref, k_ref, v_ref, g_ref, o_ref, uu_ref, acc_ref, r_ref = refs
    tk = tq
    nq = seq // tq
    heads = q_ref.shape[2] // HEAD_DIM
    nt = (((1,), (1,)), ((), ()))

    uu_ref[...] = _neg_suffix_matrix(tk)
    if has_ctx:
        uc_ref[...] = _neg_suffix_matrix(kc_ref.shape[1])
    rr = lax.broadcasted_iota(jnp.int32, (tq, tk), 0)
    cc = lax.broadcasted_iota(jnp.int32, (tq, tk), 1)
    diag_mask = cc < rr
    if own_pad:
        diag_mask = jnp.logical_and(diag_mask, cc >= own_pad)

    def hcols(hh):
        return slice(hh * HEAD_DIM, (hh + 1) * HEAD_DIM)

    def sweep(qs, blocks, uu):
        hs = range(heads)
        zs = [[lax.dot_general(qs[hh], kbs[hh], nt, preferred_element_type=F32) for hh in hs]
              for kbs, _, _ in blocks]
        css = []
        for zb, (_, _, mask) in zip(zs, blocks):
            row = []
            for z in zb:
                nl = jnp.maximum(z, 0.0) + jnp.log2(1.0 + jnp.exp2(_neg_abs(z)))
                if mask is not None:
                    nl = jnp.where(mask, nl, 0.0)
                row.append(jnp.dot(nl.astype(BF16), uu, preferred_element_type=F32))
            css.append(row)
        rs = [r_ref[hh] for hh in hs]
        avs = [None] * heads
        for zb, cb, (kbs, vbs, mask) in zip(zs, css, blocks):
            width = kbs[0].shape[0]
            for hh in hs:
                t = zb[hh] + cb[hh] + jnp.concatenate([rs[hh]] * (width // HEAD_DIM), axis=1)
                if mask is not None:
                    t = jnp.where(mask, t, NEG_BIG)
                a = jnp.exp2(t).astype(BF16)
                av = jnp.dot(a, vbs[hh], preferred_element_type=F32)
                avs[hh] = av if avs[hh] is None else avs[hh] + av
                rs[hh] = rs[hh] + jnp.broadcast_to(cb[hh][:, 0:1], (tq, HEAD_DIM))
        for hh in hs:
            r_ref[hh] = rs[hh]
            acc_ref[hh] += avs[hh]

    def kv_block(c0, mask):
        return ([k_ref[0, pl.ds(c0, tk), hcols(hh)] for hh in range(heads)],
                [v_ref[0, pl.ds(c0, tk), hcols(hh)] for hh in range(heads)], mask)

    def alive():
        return jnp.max(r_ref[...]) > DEAD_LOG2

    def q_block(qi):
        first = isinstance(qi, int) and qi == 0
        row0 = qi * tq if isinstance(qi, int) else pl.multiple_of(qi * tq, tq)
        qs = [q_ref[0, pl.ds(row0, tq), hcols(hh)] for hh in range(heads)]
        acc_ref[...] = jnp.zeros_like(acc_ref)
        r_ref[...] = jnp.zeros_like(r_ref)
        uu = uu_ref[...]

        def store_out():
            g = g_ref[...]
            for hh in range(heads):
                o_ref[0, pl.ds(row0, tq), hcols(hh)] = _rms(
                    acc_ref[hh], g[:, hcols(hh)]).astype(BF16)

        if first:
            sweep(qs, [kv_block(row0, diag_mask)], uu)
            store_out()
            more = live = alive()
        else:
            sweep(qs, [kv_block(row0, diag_mask),
                       kv_block(pl.multiple_of(row0 - tk, tk), None)], uu)
            store_out()
            more = alive()

            def kv_cond(carry):
                return jnp.logical_and(carry[0] < qi, carry[1])

            def kv_step(carry):
                n = carry[0]
                sweep(qs, [kv_block(pl.multiple_of((qi - 1 - n) * tk, tk), None)], uu)
                return n + 1, alive()

            _, live = lax.while_loop(kv_cond, kv_step, (jnp.int32(1), more))

        if has_ctx:
            @pl.when(live)
            def _():
                cw = kc_ref.shape[1]
                ctx_mask = lax.broadcasted_iota(jnp.int32, (tq, cw), 1) >= ctx_pad
                sweep(qs, [([kc_ref[0, :, hcols(hh)] for hh in range(heads)],
                            [vc_ref[0, :, hcols(hh)] for hh in range(heads)], ctx_mask)],
                      uc_ref[...])

        if has_ctx or not first:
            pl.when(more)(store_out)

    q_block(0)
    if nq > 1:
        def q_step(qi, carry):
            q_block(qi)
            return carry
        lax.fori_loop(1, nq, q_step, 0)


def _attention(q, k, v, ctx, g_ao, *, tq, heads, own_pad, ctx_pad):
    b, seq, width = q.shape
    hw = heads * HEAD_DIM
    blk = pl.BlockSpec((1, seq, hw), lambda bi, hi: (bi, 0, hi))
    in_specs = [blk, blk, blk]
    args = [q, k, v]
    scratch = [pltpu.VMEM((tq, tq), BF16)]
    if ctx is not None:
        kc, vc = ctx
        cblk = pl.BlockSpec((1, kc.shape[1], hw), lambda bi, hi: (0, 0, hi))
        in_specs += [cblk, cblk]
        args += [kc, vc]
        scratch.append(pltpu.VMEM((kc.shape[1], kc.shape[1]), BF16))
    in_specs.append(pl.BlockSpec((1, hw), lambda bi, hi: (0, hi)))
    args.append(g_ao)
    kern = functools.partial(_attn_kernel, seq=seq, tq=tq, own_pad=own_pad,
                             ctx_pad=ctx_pad if ctx is not None else None)
    return pl.pallas_call(
        kern,
        grid=(b, width // hw),
        in_specs=in_specs,
        out_specs=blk,
        out_shape=jax.ShapeDtypeStruct((b, seq, width), BF16),
        scratch_shapes=scratch + [
            pltpu.VMEM((heads, tq, HEAD_DIM), F32),
            pltpu.VMEM((heads, tq, HEAD_DIM), F32),
        ],
        compiler_params=pltpu.CompilerParams(
            dimension_semantics=("arbitrary", "arbitrary"),
            vmem_limit_bytes=VMEM_LIMIT),
        name="sb_attention",
    )(*args)


def _out_proj_kernel(h_ref, yc_ref, ya_ref, w_ref, o_ref):
    half = yc_ref.shape[1]
    o_ref[...] = (h_ref[...]
                  + jnp.dot(yc_ref[...], w_ref[0:half, :], preferred_element_type=F32)
                  + jnp.dot(ya_ref[...], w_ref[half:, :], preferred_element_type=F32))


def _out_proj(h, yc, ya, w_out, *, tm):
    m, d = h.shape
    half = yc.shape[1]
    row = pl.BlockSpec((tm, d), lambda i: (i, 0))
    yrow = pl.BlockSpec((tm, half), lambda i: (i, 0))
    return pl.pallas_call(
        _out_proj_kernel,
        grid=(m // tm,),
        in_specs=[row, yrow, yrow, pl.BlockSpec(w_out.shape, lambda i: (0, 0))],
        out_specs=row,
        out_shape=jax.ShapeDtypeStruct((m, d), F32),
        compiler_params=pltpu.CompilerParams(
            dimension_semantics=("arbitrary",), vmem_limit_bytes=VMEM_LIMIT),
        name="out_proj",
    )(h, yc, ya, w_out)


def _mlp_kernel(h_ref, g_ref, w1_ref, w2_ref, o_ref, hn_ref, *, sub):
    f = pl.program_id(1)

    def ff_tile(hn):
        hid = jnp.dot(hn, w1_ref[...], preferred_element_type=F32)
        hid = jnp.square(jnp.maximum(hid, 0.0)).astype(BF16)
        return jnp.dot(hid, w2_ref[...], preferred_element_type=F32)

    @pl.when(f == 0)
    def _():
        for r in range(0, h_ref.shape[0], sub):
            rows = slice(r, r + sub)
            x = h_ref[rows, :]
            hn = _rms(x, g_ref[...]).astype(BF16)
            hn_ref[rows, :] = hn
            o_ref[rows, :] = x + ff_tile(hn)

    @pl.when(f > 0)
    def _():
        o_ref[...] += ff_tile(hn_ref[...])


def _mlp(h, g_mlp, w1, w2, *, tm, tf):
    m, d = h.shape
    d_ff = w1.shape[1]
    row = pl.BlockSpec((tm, d), lambda i, f: (i, 0))
    return pl.pallas_call(
        functools.partial(_mlp_kernel, sub=min(tm, MLP_FIRST_SUB_ROWS)),
        grid=(m // tm, d_ff // tf),
        in_specs=[row,
                  pl.BlockSpec((1, d), lambda i, f: (0, 0)),
                  pl.BlockSpec((d, tf), lambda i, f: (0, f)),
                  pl.BlockSpec((tf, d), lambda i, f: (f, 0))],
        out_specs=row,
        out_shape=jax.ShapeDtypeStruct((m, d), F32),
        scratch_shapes=[pltpu.VMEM((tm, d), BF16)],
        compiler_params=pltpu.CompilerParams(
            dimension_semantics=("arbitrary", "arbitrary"), vmem_limit_bytes=VMEM_LIMIT),
        name="mlp",
    )(h, g_mlp, w1, w2)


def _tiles(seq):
    return dict(tm_in=min(512, seq), tm_out=min(512, seq), tm_mlp=min(1024, seq), tf=1024,
                tq=min(256, seq), heads=4)


def kernel(x, meta_tokens, g_mix, w_in, conv_w, g_q, g_k, g_conv_out, g_attn_out, w_out,
           g_mlp, w_mlp_in, w_mlp_out):
    b, seq, d = x.shape
    depth = w_in.shape[0]
    n_meta = meta_tokens.shape[0]
    width = conv_w.shape[2]
    assert n_meta <= META_ROWS and seq % META_ROWS == 0
    cfg = _tiles(seq)
    meta_pad = META_ROWS - n_meta

    h = x.reshape(b * seq, d)
    hm = jnp.pad(meta_tokens.astype(x.dtype), ((meta_pad, 0), (0, 0)))
    zero_tail = jnp.zeros((SUBLANES, width), F32)
    later_w = (w_mlp_in, w_mlp_out, w_out)
    ride = all(_can_ride(a, b * seq // cfg["tm_in"]) for a in later_w + (w_in,))
    w_in_i = w_in[0].astype(BF16)

    for i in range(depth):
        g_mix_i = g_mix[i].reshape(1, d)
        g_co_i = g_conv_out[i].reshape(1, width)
        g_ao_i = g_attn_out[i].reshape(1, width)
        g_mlp_i = g_mlp[i].reshape(1, d)
        g_qk_i = jnp.stack([g_q[i] * (HEAD_DIM ** -0.5 * LOG2E), g_k[i]])
        last = i + 1 == depth
        riders = []
        if ride:
            riders = [(a, i) for a in later_w] + ([] if last else [(w_in, i + 1)])

        ycm, qm, km, vm, tail_m = _in_proj(hm, g_mix_i, w_in_i, conv_w[i], g_co_i, g_qk_i,
                                           zero_tail, tm=META_ROWS, tiles_per_seq=1)
        yc, q, k, v, _, *cast = _in_proj(h, g_mix_i, w_in_i, conv_w[i], g_co_i, g_qk_i,
                                         tail_m[0], riders,
                                         tm=cfg["tm_in"], tiles_per_seq=seq // cfg["tm_in"])
        if not ride:
            cast = [a[i].astype(BF16) for a in later_w] + ([] if last else [w_in[i + 1].astype(BF16)])
        w1_b, w2_b, w_out_b = cast[:3]
        ya = _attention(q.reshape(b, seq, width), k.reshape(b, seq, width),
                        v.reshape(b, seq, width), (km[None], vm[None]), g_ao_i,
                        tq=cfg["tq"], heads=cfg["heads"], own_pad=0, ctx_pad=meta_pad)
        h = _out_proj(h, yc, ya.reshape(b * seq, width), w_out_b, tm=cfg["tm_out"])
        h = _mlp(h, g_mlp_i, w1_b, w2_b, tm=cfg["tm_mlp"], tf=cfg["tf"])

        if not last:
            yam = _attention(qm[None], km[None], vm[None], None, g_ao_i,
                             tq=META_ROWS, heads=cfg["heads"], own_pad=meta_pad, ctx_pad=None)
            hm = _out_proj(hm, ycm, yam[0], w_out_b, tm=META_ROWS)
            hm = _mlp(hm, g_mlp_i, w1_b, w2_b, tm=META_ROWS, tf=cfg["tf"])
            w_in_i = cast[3]

    return h.reshape(b, seq, d)
```

```python
import functools

import jax
import jax.numpy as jnp
from jax import lax
from jax.experimental import pallas as pl
from jax.experimental.pallas import tpu as pltpu

F32 = jnp.float32
BF16 = jnp.bfloat16

EPS = 1e-6
HEAD_DIM = 128
CONV_K = 3
SUBLANES = 8
BF16_ROWS = 16
META_ROWS = 128
COL_CHUNK = 2 * HEAD_DIM
MLP_FIRST_SUB_ROWS = 256
VMEM_LIMIT = 60 * 1024 * 1024
NEG_BIG = -1e30
LOG2E = 1.4426950408889634
DEAD_LOG2 = -160.0


def _rms(x, g):
    ms = jnp.mean(x * x, axis=-1, keepdims=True)
    return x * lax.rsqrt(ms + EPS) * g


def _in_proj_kernel(x_ref, g_ref, w_ref, cw_ref, gco_ref, gqk_ref, ctx_ref, *rest,
                    tm, tiles_per_seq, n_riders):
    rider_in = rest[:n_riders]
    yc_ref, q_ref, k_ref, v_ref, tail_ref = rest[n_riders:n_riders + 5]
    rider_out = rest[n_riders + 5:2 * n_riders + 5]
    hn_ref, carry_ref = rest[2 * n_riders + 5:]
    step = pl.program_id(0)
    cc = COL_CHUNK
    width = yc_ref.shape[1]
    n_chunks = width // cc

    def cast_riders():
        for src, dst in zip(rider_in, rider_out):
            dst[...] = src[...].astype(BF16)

    @pl.when(step == 0)
    def _():
        hn_ref[0] = _rms(x_ref[...], g_ref[...]).astype(BF16)
        cast_riders()

    @pl.when(jnp.logical_and(step > 0, ((step - 1) % tiles_per_seq) == 0))
    def _():
        carry_ref[...] = ctx_ref[...]

    @pl.when(step > 0)
    def _tile():
        slot = (step - 1) % 2

        def proj(section, c):
            col = section * width + c * cc
            return jnp.dot(hn_ref[slot], w_ref[:, col:col + cc], preferred_element_type=F32)

        for c in range(n_chunks):
            cs = slice(c * cc, (c + 1) * cc)
            gate_b = proj(0, c)
            cu = proj(1, c) * proj(2, c)
            if c == 0:
                hn_ref[1 - slot] = _rms(x_ref[...], g_ref[...]).astype(BF16)
            prev = carry_ref[:, cs]
            tail = cu[tm - SUBLANES:tm, :]
            carry_ref[:, cs] = tail
            tail_ref[0, :, cs] = tail
            row = lax.broadcasted_iota(jnp.int32, (tm, cc), 0)
            hist1 = jnp.broadcast_to(prev[SUBLANES - 1:SUBLANES, :], (tm, cc))
            hist2 = jnp.broadcast_to(prev[SUBLANES - 2:SUBLANES - 1, :], (tm, cc))
            c1 = jnp.where(row >= 1, pltpu.roll(cu, 1, axis=0), hist1)
            c2 = jnp.where(row >= 2, pltpu.roll(cu, 2, axis=0),
                           jnp.where(row == 1, hist1, hist2))
            w = cw_ref[:, cs]
            y = gate_b * (w[0:1, :] * c2 + w[1:2, :] * c1 + w[2:3, :] * cu)
            for s in range(c * cc, (c + 1) * cc, HEAD_DIM):
                sl = slice(s, s + HEAD_DIM)
                yc_ref[:, sl] = _rms(y[:, s - c * cc:s - c * cc + HEAD_DIM],
                                     gco_ref[:, sl]).astype(BF16)

        gqk = gqk_ref[...]
        for section, o_ref, gain in ((3, q_ref, gqk[0:1, :]), (4, k_ref, gqk[1:2, :])):
            for c in range(n_chunks):
                acc = proj(section, c)
                if section == 3 and c == 1:
                    cast_riders()
                for s in range(0, cc, HEAD_DIM):
                    o_ref[:, c * cc + s:c * cc + s + HEAD_DIM] = _rms(
                        acc[:, s:s + HEAD_DIM], gain).astype(BF16)
        for c in range(n_chunks):
            v_ref[:, c * cc:(c + 1) * cc] = proj(5, c).astype(BF16)


def _can_ride(a, n_tiles):
    return a.shape[1] % (n_tiles * BF16_ROWS) == 0


def _in_proj(h, g_mix, w, conv_w, g_co, g_qk, ctx_tail, riders=(), *, tm, tiles_per_seq):
    m, d = h.shape
    width = conv_w.shape[1]
    n_tiles = m // tm
    assert w.shape == (d, 6 * width) and width % COL_CHUNK == 0 and m % tm == 0
    assert all(_can_ride(a, n_tiles) for a, _ in riders)

    kern = functools.partial(_in_proj_kernel, tm=tm, tiles_per_seq=tiles_per_seq,
                             n_riders=len(riders))
    act = jax.ShapeDtypeStruct((m, width), BF16)
    row = pl.BlockSpec((tm, width), lambda s: (jnp.maximum(s - 1, 0), 0))
    const = lambda s: (0, 0)
    slab = lambda s: jnp.minimum(s, n_tiles - 1)
    return pl.pallas_call(
        kern,
        grid=(n_tiles + 1,),
        in_specs=[
            pl.BlockSpec((tm, d), lambda s: (slab(s), 0)),
            pl.BlockSpec((1, d), const),
            pl.BlockSpec(w.shape, const, pipeline_mode=pl.Buffered(1)),
            pl.BlockSpec((CONV_K, width), const),
            pl.BlockSpec((1, width), const),
            pl.BlockSpec((2, HEAD_DIM), const),
            pl.BlockSpec((SUBLANES, width), const),
        ] + [pl.BlockSpec((None, a.shape[1] // n_tiles, a.shape[2]),
                          functools.partial(lambda layer, s: (layer, slab(s), 0), layer))
             for a, layer in riders],
        out_specs=[row, row, row, row,
                   pl.BlockSpec((1, SUBLANES, width), lambda s: (jnp.maximum(s - 1, 0), 0, 0))]
        + [pl.BlockSpec((a.shape[1] // n_tiles, a.shape[2]), lambda s: (slab(s), 0))
           for a, _ in riders],
        out_shape=[act, act, act, act,
                   jax.ShapeDtypeStruct((n_tiles, SUBLANES, width), F32)]
        + [jax.ShapeDtypeStruct(a.shape[1:], BF16) for a, _ in riders],
        scratch_shapes=[
            pltpu.VMEM((2, tm, d), BF16),
            pltpu.VMEM((SUBLANES, width), F32),
        ],
        compiler_params=pltpu.CompilerParams(
            dimension_semantics=("arbitrary",),
            vmem_limit_bytes=VMEM_LIMIT),
        name="in_proj",
    )(h, g_mix, w, conv_w, g_co, g_qk, ctx_tail, *[a for a, _ in riders])


def _neg_abs(x):
    bits = lax.bitcast_convert_type(x, jnp.int32) | jnp.int32(-2 ** 31)
    return lax.bitcast_convert_type(bits, F32)


def _neg_suffix_matrix(n):
    rr = lax.broadcasted_iota(jnp.int32, (n, n), 0)
    cc = lax.broadcasted_iota(jnp.int32, (n, n), 1)
    return jnp.where(rr >= cc, -1.0, 0.0).astype(BF16)


def _attn_kernel(*refs, seq, tq, own_pad, ctx_pad):
    has_ctx = ctx_pad is not None
    if has_ctx:
        (q_ref, k_ref, v_ref, kc_ref, vc_ref, g_ref, o_ref,
         uu_ref, uc_ref, acc_ref, r_ref) = refs
    else:
        q_ref, k_ref, v_ref, g_ref, o_ref, uu_ref, acc_ref, r_ref = refs
    tk = tq
    nq = seq // tq
    heads = q_ref.shape[2] // HEAD_DIM
    nt = (((1,), (1,)), ((), ()))

    uu_ref[...] = _neg_suffix_matrix(tk)
    if has_ctx:
        uc_ref[...] = _neg_suffix_matrix(kc_ref.shape[1])
    rr = lax.broadcasted_iota(jnp.int32, (tq, tk), 0)
    cc = lax.broadcasted_iota(jnp.int32, (tq, tk), 1)
    diag_mask = cc < rr
    if own_pad:
        diag_mask = jnp.logical_and(diag_mask, cc >= own_pad)

    def hcols(hh):
        return slice(hh * HEAD_DIM, (hh + 1) * HEAD_DIM)

    def sweep(qs, blocks, uu):
        hs = range(heads)
        zs = [[lax.dot_general(qs[hh], kbs[hh], nt, preferred_element_type=F32) for hh in hs]
              for kbs, _, _ in blocks]
        css = []
        for zb, (_, _, mask) in zip(zs, blocks):
            row = []
            for z in zb:
                nl = jnp.maximum(z, 0.0) + jnp.log2(1.0 + jnp.exp2(_neg_abs(z)))
                if mask is not None:
                    nl = jnp.where(mask, nl, 0.0)
                row.append(jnp.dot(nl.astype(BF16), uu, preferred_element_type=F32))
            css.append(row)
        rs = [r_ref[hh] for hh in hs]
        avs = [None] * heads
        for zb, cb, (kbs, vbs, mask) in zip(zs, css, blocks):
            width = kbs[0].shape[0]
            for hh in hs:
                t = zb[hh] + cb[hh] + jnp.concatenate([rs[hh]] * (width // HEAD_DIM), axis=1)
                if mask is not None:
                    t = jnp.where(mask, t, NEG_BIG)
                a = jnp.exp2(t).astype(BF16)
                av = jnp.dot(a, vbs[hh], preferred_element_type=F32)
                avs[hh] = av if avs[hh] is None else avs[hh] + av
                rs[hh] = rs[hh] + jnp.broadcast_to(cb[hh][:, 0:1], (tq, HEAD_DIM))
        for hh in hs:
            r_ref[hh] = rs[hh]
            acc_ref[hh] += avs[hh]

    def kv_block(c0, mask):
        return ([k_ref[0, pl.ds(c0, tk), hcols(hh)] for hh in range(heads)],
                [v_ref[0, pl.ds(c0, tk), hcols(hh)] for hh in range(heads)], mask)

    def alive():
        return jnp.max(r_ref[...]) > DEAD_LOG2

    def q_block(qi):
        first = isinstance(qi, int) and qi == 0
        row0 = qi * tq if isinstance(qi, int) else pl.multiple_of(qi * tq, tq)
        qs = [q_ref[0, pl.ds(row0, tq), hcols(hh)] for hh in range(heads)]
        acc_ref[...] = jnp.zeros_like(acc_ref)
        r_ref[...] = jnp.zeros_like(r_ref)
        uu = uu_ref[...]

        def store_out():
            g = g_ref[...]
            for hh in range(heads):
                o_ref[0, pl.ds(row0, tq), hcols(hh)] = _rms(
                    acc_ref[hh], g[:, hcols(hh)]).astype(BF16)

        if first:
            sweep(qs, [kv_block(row0, diag_mask)], uu)
            store_out()
            more = live = alive()
        else:
            sweep(qs, [kv_block(row0, diag_mask),
                       kv_block(pl.multiple_of(row0 - tk, tk), None)], uu)
            store_out()
            more = alive()

            def kv_cond(carry):
                return jnp.logical_and(carry[0] < qi, carry[1])

            def kv_step(carry):
                n = carry[0]
                sweep(qs, [kv_block(pl.multiple_of((qi - 1 - n) * tk, tk), None)], uu)
                return n + 1, alive()

            _, live = lax.while_loop(kv_cond, kv_step, (jnp.int32(1), more))

        if has_ctx:
            @pl.when(live)
            def _():
                cw = kc_ref.shape[1]
                ctx_mask = lax.broadcasted_iota(jnp.int32, (tq, cw), 1) >= ctx_pad
                sweep(qs, [([kc_ref[0, :, hcols(hh)] for hh in range(heads)],
                            [vc_ref[0, :, hcols(hh)] for hh in range(heads)], ctx_mask)],
                      uc_ref[...])

        if has_ctx or not first:
            pl.when(more)(store_out)

    q_block(0)
    if nq > 1:
        def q_step(qi, carry):
            q_block(qi)
            return carry
        lax.fori_loop(1, nq, q_step, 0)


def _attention(q, k, v, ctx, g_ao, *, tq, heads, own_pad, ctx_pad):
    b, seq, width = q.shape
    hw = heads * HEAD_DIM
    blk = pl.BlockSpec((1, seq, hw), lambda bi, hi: (bi, 0, hi))
    in_specs = [blk, blk, blk]
    args = [q, k, v]
    scratch = [pltpu.VMEM((tq, tq), BF16)]
    if ctx is not None:
        kc, vc = ctx
        cblk = pl.BlockSpec((1, kc.shape[1], hw), lambda bi, hi: (0, 0, hi))
        in_specs += [cblk, cblk]
        args += [kc, vc]
        scratch.append(pltpu.VMEM((kc.shape[1], kc.shape[1]), BF16))
    in_specs.append(pl.BlockSpec((1, hw), lambda bi, hi: (0, hi)))
    args.append(g_ao)
    kern = functools.partial(_attn_kernel, seq=seq, tq=tq, own_pad=own_pad,
                             ctx_pad=ctx_pad if ctx is not None else None)
    return pl.pallas_call(
        kern,
        grid=(b, width // hw),
        in_specs=in_specs,
        out_specs=blk,
        out_shape=jax.ShapeDtypeStruct((b, seq, width), BF16),
        scratch_shapes=scratch + [
            pltpu.VMEM((heads, tq, HEAD_DIM), F32),
            pltpu.VMEM((heads, tq, HEAD_DIM), F32),
        ],
        compiler_params=pltpu.CompilerParams(
            dimension_semantics=("arbitrary", "arbitrary"),
            vmem_limit_bytes=VMEM_LIMIT),
        name="sb_attention",
    )(*args)


def _out_proj_kernel(h_ref, yc_ref, ya_ref, w_ref, o_ref):
    half = yc_ref.shape[1]
    o_ref[...] = (h_ref[...]
                  + jnp.dot(yc_ref[...], w_ref[0:half, :], preferred_element_type=F32)
                  + jnp.dot(ya_ref[...], w_ref[half:, :], preferred_element_type=F32))


def _out_proj(h, yc, ya, w_out, *, tm):
    m, d = h.shape
    half = yc.shape[1]
    row = pl.BlockSpec((tm, d), lambda i: (i, 0))
    yrow = pl.BlockSpec((tm, half), lambda i: (i, 0))
    return pl.pallas_call(
        _out_proj_kernel,
        grid=(m // tm,),
        in_specs=[row, yrow, yrow, pl.BlockSpec(w_out.shape, lambda i: (0, 0))],
        out_specs=row,
        out_shape=jax.ShapeDtypeStruct((m, d), F32),
        compiler_params=pltpu.CompilerParams(
            dimension_semantics=("arbitrary",), vmem_limit_bytes=VMEM_LIMIT),
        name="out_proj",
    )(h, yc, ya, w_out)


def _mlp_kernel(h_ref, g_ref, w1_ref, w2_ref, o_ref, hn_ref, *, sub):
    f = pl.program_id(1)

    def ff_tile(hn):
        hid = jnp.dot(hn, w1_ref[...], preferred_element_type=F32)
        hid = jnp.square(jnp.maximum(hid, 0.0)).astype(BF16)
        return jnp.dot(hid, w2_ref[...], preferred_element_type=F32)

    @pl.when(f == 0)
    def _():
        for r in range(0, h_ref.shape[0], sub):
            rows = slice(r, r + sub)
            x = h_ref[rows, :]
            hn = _rms(x, g_ref[...]).astype(BF16)
            hn_ref[rows, :] = hn
            o_ref[rows, :] = x + ff_tile(hn)

    @pl.when(f > 0)
    def _():
        o_ref[...] += ff_tile(hn_ref[...])


def _mlp(h, g_mlp, w1, w2, *, tm, tf):
    m, d = h.shape
    d_ff = w1.shape[1]
    row = pl.BlockSpec((tm, d), lambda i, f: (i, 0))
    return pl.pallas_call(
        functools.partial(_mlp_kernel, sub=min(tm, MLP_FIRST_SUB_ROWS)),
        grid=(m // tm, d_ff // tf),
        in_specs=[row,
                  pl.BlockSpec((1, d), lambda i, f: (0, 0)),
                  pl.BlockSpec((d, tf), lambda i, f: (0, f)),
                  pl.BlockSpec((tf, d), lambda i, f: (f, 0))],
        out_specs=row,
        out_shape=jax.ShapeDtypeStruct((m, d), F32),
        scratch_shapes=[pltpu.VMEM((tm, d), BF16)],
        compiler_params=pltpu.CompilerParams(
            dimension_semantics=("arbitrary", "arbitrary"), vmem_limit_bytes=VMEM_LIMIT),
        name="mlp",
    )(h, g_mlp, w1, w2)


def _tiles(seq):
    return dict(tm_in=min(512, seq), tm_out=min(512, seq), tm_mlp=min(1024, seq), tf=1024,
                tq=min(256, seq), heads=4)


def kernel(x, meta_tokens, g_mix, w_in, conv_w, g_q, g_k, g_conv_out, g_attn_out, w_out,
           g_mlp, w_mlp_in, w_mlp_out):
    b, seq, d = x.shape
    depth = w_in.shape[0]
    n_meta = meta_tokens.shape[0]
    width = conv_w.shape[2]
    assert n_meta <= META_ROWS and seq % META_ROWS == 0
    cfg = _tiles(seq)
    meta_pad = META_ROWS - n_meta

    h = x.reshape(b * seq, d)
    hm = jnp.pad(meta_tokens.astype(x.dtype), ((meta_pad, 0), (0, 0)))
    zero_tail = jnp.zeros((SUBLANES, width), F32)
    later_w = (w_mlp_in, w_mlp_out, w_out)
    ride = all(_can_ride(a, b * seq // cfg["tm_in"]) for a in later_w + (w_in,))
    w_in_i = w_in[0].astype(BF16)

    for i in range(depth):
        g_mix_i = g_mix[i].reshape(1, d)
        g_co_i = g_conv_out[i].reshape(1, width)
        g_ao_i = g_attn_out[i].reshape(1, width)
        g_mlp_i = g_mlp[i].reshape(1, d)
        g_qk_i = jnp.stack([g_q[i] * (HEAD_DIM ** -0.5 * LOG2E), g_k[i]])
        last = i + 1 == depth
        riders = []
        if ride:
            riders = [(a, i) for a in later_w] + ([] if last else [(w_in, i + 1)])

        ycm, qm, km, vm, tail_m = _in_proj(hm, g_mix_i, w_in_i, conv_w[i], g_co_i, g_qk_i,
                                           zero_tail, tm=META_ROWS, tiles_per_seq=1)
        yc, q, k, v, _, *cast = _in_proj(h, g_mix_i, w_in_i, conv_w[i], g_co_i, g_qk_i,
                                         tail_m[0], riders,
                                         tm=cfg["tm_in"], tiles_per_seq=seq // cfg["tm_in"])
        if not ride:
            cast = [a[i].astype(BF16) for a in later_w] + ([] if last else [w_in[i + 1].astype(BF16)])
        w1_b, w2_b, w_out_b = cast[:3]
        ya = _attention(q.reshape(b, seq, width), k.reshape(b, seq, width),
                        v.reshape(b, seq, width), (km[None], vm[None]), g_ao_i,
                        tq=cfg["tq"], heads=cfg["heads"], own_pad=0, ctx_pad=meta_pad)
        h = _out_proj(h, yc, ya.reshape(b * seq, width), w_out_b, tm=cfg["tm_out"])
        h = _mlp(h, g_mlp_i, w1_b, w2_b, tm=cfg["tm_mlp"], tf=cfg["tf"])

        if not last:
            yam = _attention(qm[None], km[None], vm[None], None, g_ao_i,
                             tq=META_ROWS, heads=cfg["heads"], own_pad=meta_pad, ctx_pad=None)
            hm = _out_proj(hm, ycm, yam[0], w_out_b, tm=META_ROWS)
            hm = _mlp(hm, g_mlp_i, w1_b, w2_b, tm=META_ROWS, tf=cfg["tf"])
            w_in_i = cast[3]

    return h.reshape(b, seq, d)
```

```python
import functools

import jax
import jax.numpy as jnp
from jax import lax
from jax.experimental import pallas as pl
from jax.experimental.pallas import tpu as pltpu

F32 = jnp.float32
BF16 = jnp.bfloat16

EPS = 1e-6
HEAD_DIM = 128
CONV_K = 3
SUBLANES = 8
BF16_ROWS = 16
META_ROWS = 128
COL_CHUNK = 2 * HEAD_DIM
MLP_FIRST_SUB_ROWS = 256
VMEM_LIMIT = 60 * 1024 * 1024
NEG_BIG = -1e30
LOG2E = 1.4426950408889634
DEAD_LOG2 = -160.0


def _rms(x, g):
    ms = jnp.mean(x * x, axis=-1, keepdims=True)
    return x * lax.rsqrt(ms + EPS) * g


def _in_proj_kernel(x_ref, g_ref, w_ref, cw_ref, gco_ref, gqk_ref, ctx_ref, *rest,
                    tm, tiles_per_seq, n_riders):
    rider_in = rest[:n_riders]
    yc_ref, q_ref, k_ref, v_ref, tail_ref = rest[n_riders:n_riders + 5]
    rider_out = rest[n_riders + 5:2 * n_riders + 5]
    hn_ref, carry_ref = rest[2 * n_riders + 5:]
    step = pl.program_id(0)
    cc = COL_CHUNK
    width = yc_ref.shape[1]
    n_chunks = width // cc

    def cast_riders():
        for src, dst in zip(rider_in, rider_out):
            dst[...] = src[...].astype(BF16)

    @pl.when(step == 0)
    def _():
        hn_ref[0] = _rms(x_ref[...], g_ref[...]).astype(BF16)
        cast_riders()

    @pl.when(jnp.logical_and(step > 0, ((step - 1) % tiles_per_seq) == 0))
    def _():
        carry_ref[...] = ctx_ref[...]

    @pl.when(step > 0)
    def _tile():
        slot = (step - 1) % 2

        def proj(section, c):
            col = section * width + c * cc
            return jnp.dot(hn_ref[slot], w_ref[:, col:col + cc], preferred_element_type=F32)

        for c in range(n_chunks):
            cs = slice(c * cc, (c + 1) * cc)
            gate_b = proj(0, c)
            cu = proj(1, c) * proj(2, c)
            if c == 0:
                hn_ref[1 - slot] = _rms(x_ref[...], g_ref[...]).astype(BF16)
                cast_riders()
            prev = carry_ref[:, cs]
            tail = cu[tm - SUBLANES:tm, :]
            carry_ref[:, cs] = tail
            tail_ref[0, :, cs] = tail
            row = lax.broadcasted_iota(jnp.int32, (tm, cc), 0)
            hist1 = jnp.broadcast_to(prev[SUBLANES - 1:SUBLANES, :], (tm, cc))
            hist2 = jnp.broadcast_to(prev[SUBLANES - 2:SUBLANES - 1, :], (tm, cc))
            c1 = jnp.where(row >= 1, pltpu.roll(cu, 1, axis=0), hist1)
            c2 = jnp.where(row >= 2, pltpu.roll(cu, 2, axis=0),
                           jnp.where(row == 1, hist1, hist2))
            w = cw_ref[:, cs]
            y = gate_b * (w[0:1, :] * c2 + w[1:2, :] * c1 + w[2:3, :] * cu)
            for s in range(c * cc, (c + 1) * cc, HEAD_DIM):
                sl = slice(s, s + HEAD_DIM)
                yc_ref[:, sl] = _rms(y[:, s - c * cc:s - c * cc + HEAD_DIM],
                                     gco_ref[:, sl]).astype(BF16)

        gqk = gqk_ref[...]
        for section, o_ref, gain in ((3, q_ref, gqk[0:1, :]), (4, k_ref, gqk[1:2, :])):
            for c in range(n_chunks):
                acc = proj(section, c)
                for s in range(0, cc, HEAD_DIM):
                    o_ref[:, c * cc + s:c * cc + s + HEAD_DIM] = _rms(
                        acc[:, s:s + HEAD_DIM], gain).astype(BF16)
        for c in range(n_chunks):
            v_ref[:, c * cc:(c + 1) * cc] = proj(5, c).astype(BF16)


def _can_ride(a, n_tiles):
    return a.shape[1] % (n_tiles * BF16_ROWS) == 0


def _in_proj(h, g_mix, w, conv_w, g_co, g_qk, ctx_tail, riders=(), *, tm, tiles_per_seq):
    m, d = h.shape
    width = conv_w.shape[1]
    n_tiles = m // tm
    assert w.shape == (d, 6 * width) and width % COL_CHUNK == 0 and m % tm == 0
    assert all(_can_ride(a, n_tiles) for a, _ in riders)

    kern = functools.partial(_in_proj_kernel, tm=tm, tiles_per_seq=tiles_per_seq,
                             n_riders=len(riders))
    act = jax.ShapeDtypeStruct((m, width), BF16)
    row = pl.BlockSpec((tm, width), lambda s: (jnp.maximum(s - 1, 0), 0))
    const = lambda s: (0, 0)
    slab = lambda s: jnp.minimum(s, n_tiles - 1)
    return pl.pallas_call(
        kern,
        grid=(n_tiles + 1,),
        in_specs=[
            pl.BlockSpec((tm, d), lambda s: (slab(s), 0)),
            pl.BlockSpec((1, d), const),
            pl.BlockSpec(w.shape, const, pipeline_mode=pl.Buffered(1)),
            pl.BlockSpec((CONV_K, width), const),
            pl.BlockSpec((1, width), const),
            pl.BlockSpec((2, HEAD_DIM), const),
            pl.BlockSpec((SUBLANES, width), const),
        ] + [pl.BlockSpec((None, a.shape[1] // n_tiles, a.shape[2]),
                          functools.partial(lambda layer, s: (layer, slab(s), 0), layer))
             for a, layer in riders],
        out_specs=[row, row, row, row,
                   pl.BlockSpec((1, SUBLANES, width), lambda s: (jnp.maximum(s - 1, 0), 0, 0))]
        + [pl.BlockSpec((a.shape[1] // n_tiles, a.shape[2]), lambda s: (slab(s), 0))
           for a, _ in riders],
        out_shape=[act, act, act, act,
                   jax.ShapeDtypeStruct((n_tiles, SUBLANES, width), F32)]
        + [jax.ShapeDtypeStruct(a.shape[1:], BF16) for a, _ in riders],
        scratch_shapes=[
            pltpu.VMEM((2, tm, d), BF16),
            pltpu.VMEM((SUBLANES, width), F32),
        ],
        compiler_params=pltpu.CompilerParams(
            dimension_semantics=("arbitrary",),
            vmem_limit_bytes=VMEM_LIMIT),
        name="in_proj",
    )(h, g_mix, w, conv_w, g_co, g_qk, ctx_tail, *[a for a, _ in riders])


def _neg_abs(x):
    bits = lax.bitcast_convert_type(x, jnp.int32) | jnp.int32(-2 ** 31)
    return lax.bitcast_convert_type(bits, F32)


def _neg_suffix_matrix(n):
    rr = lax.broadcasted_iota(jnp.int32, (n, n), 0)
    cc = lax.broadcasted_iota(jnp.int32, (n, n), 1)
    return jnp.where(rr >= cc, -1.0, 0.0).astype(BF16)


def _attn_kernel(*refs, seq, tq, own_pad, ctx_pad):
    has_ctx = ctx_pad is not None
    if has_ctx:
        (q_ref, k_ref, v_ref, kc_ref, vc_ref, g_ref, o_ref,
         uu_ref, uc_ref, acc_ref, r_ref) = refs
    else:
        q_ref, k_ref, v_ref, g_ref, o_ref, uu_ref, acc_ref, r_ref = refs
    tk = tq
    nq = seq // tq
    heads = q_ref.shape[2] // HEAD_DIM
    nt = (((1,), (1,)), ((), ()))

    uu_ref[...] = _neg_suffix_matrix(tk)
    if has_ctx:
        uc_ref[...] = _neg_suffix_matrix(kc_ref.shape[1])
    rr = lax.broadcasted_iota(jnp.int32, (tq, tk), 0)
    cc = lax.broadcasted_iota(jnp.int32, (tq, tk), 1)
    diag_mask = cc < rr
    if own_pad:
        diag_mask = jnp.logical_and(diag_mask, cc >= own_pad)

    def hcols(hh):
        return slice(hh * HEAD_DIM, (hh + 1) * HEAD_DIM)

    def sweep(qs, blocks, uu):
        hs = range(heads)
        zs = [[lax.dot_general(qs[hh](), kbs[hh](), nt, preferred_element_type=F32) for hh in hs]
              for kbs, _, _ in blocks]
        css = []
        for zb, (_, _, mask) in zip(zs, blocks):
            row = []
            for z in zb:
                nl = jnp.maximum(z, 0.0) + jnp.log2(1.0 + jnp.exp2(_neg_abs(z)))
                if mask is not None:
                    nl = jnp.where(mask, nl, 0.0)
                row.append(jnp.dot(nl.astype(BF16), uu[...], preferred_element_type=F32))
            css.append(row)
        rs = [r_ref[hh] for hh in hs]
        avs = [None] * heads
        for zb, cb, (kbs, vbs, mask) in zip(zs, css, blocks):
            width = uu.shape[0]
            for hh in hs:
                t = zb[hh] + cb[hh] + jnp.concatenate([rs[hh]] * (width // HEAD_DIM), axis=1)
                if mask is not None:
                    t = jnp.where(mask, t, NEG_BIG)
                a = jnp.exp2(t).astype(BF16)
                av = jnp.dot(a, vbs[hh](), preferred_element_type=F32)
                avs[hh] = av if avs[hh] is None else avs[hh] + av
                rs[hh] = rs[hh] + jnp.broadcast_to(cb[hh][:, 0:1], (tq, HEAD_DIM))
        for hh in hs:
            r_ref[hh] = rs[hh]
            acc_ref[hh] += avs[hh]

    def kv_block(c0, mask):
        return ([lambda hh=hh: k_ref[0, pl.ds(c0, tk), hcols(hh)] for hh in range(heads)],
                [lambda hh=hh: v_ref[0, pl.ds(c0, tk), hcols(hh)] for hh in range(heads)], mask)

    def alive():
        return jnp.max(r_ref[...]) > DEAD_LOG2

    def q_block(qi):
        first = isinstance(qi, int) and qi == 0
        row0 = qi * tq if isinstance(qi, int) else pl.multiple_of(qi * tq, tq)
        qs = [lambda hh=hh: q_ref[0, pl.ds(row0, tq), hcols(hh)] for hh in range(heads)]
        acc_ref[...] = jnp.zeros_like(acc_ref)
        r_ref[...] = jnp.zeros_like(r_ref)
        uu = uu_ref

        def store_out():
            g = g_ref[...]
            for hh in range(heads):
                o_ref[0, pl.ds(row0, tq), hcols(hh)] = _rms(
                    acc_ref[hh], g[:, hcols(hh)]).astype(BF16)

        if first:
            sweep(qs, [kv_block(row0, diag_mask)], uu)
            store_out()
            more = live = alive()
        else:
            sweep(qs, [kv_block(row0, diag_mask),
                       kv_block(pl.multiple_of(row0 - tk, tk), None)], uu)
            store_out()
            more = alive()

            def kv_cond(carry):
                return jnp.logical_and(carry[0] < qi, carry[1])

            def kv_step(carry):
                n = carry[0]
                sweep(qs, [kv_block(pl.multiple_of((qi - 1 - n) * tk, tk), None)], uu)
                return n + 1, alive()

            _, live = lax.while_loop(kv_cond, kv_step, (jnp.int32(1), more))

        if has_ctx:
            @pl.when(live)
            def _():
                cw = kc_ref.shape[1]
                ctx_mask = lax.broadcasted_iota(jnp.int32, (tq, cw), 1) >= ctx_pad
                sweep(qs, [([lambda hh=hh: kc_ref[0, :, hcols(hh)] for hh in range(heads)],
                            [lambda hh=hh: vc_ref[0, :, hcols(hh)] for hh in range(heads)],
                            ctx_mask)], uc_ref)

        if has_ctx or not first:
            pl.when(more)(store_out)

    q_block(0)
    if nq > 1:
        def q_step(qi, carry):
            q_block(qi)
            return carry
        lax.fori_loop(1, nq, q_step, 0)


def _attention(q, k, v, ctx, g_ao, *, tq, heads, own_pad, ctx_pad):
    b, seq, width = q.shape
    hw = heads * HEAD_DIM
    blk = pl.BlockSpec((1, seq, hw), lambda bi, hi: (bi, 0, hi))
    in_specs = [blk, blk, blk]
    args = [q, k, v]
    scratch = [pltpu.VMEM((tq, tq), BF16)]
    if ctx is not None:
        kc, vc = ctx
        cblk = pl.BlockSpec((1, kc.shape[1], hw), lambda bi, hi: (0, 0, hi))
        in_specs += [cblk, cblk]
        args += [kc, vc]
        scratch.append(pltpu.VMEM((kc.shape[1], kc.shape[1]), BF16))
    in_specs.append(pl.BlockSpec((1, hw), lambda bi, hi: (0, hi)))
    args.append(g_ao)
    kern = functools.partial(_attn_kernel, seq=seq, tq=tq, own_pad=own_pad,
                             ctx_pad=ctx_pad if ctx is not None else None)
    return pl.pallas_call(
        kern,
        grid=(b, width // hw),
        in_specs=in_specs,
        out_specs=blk,
        out_shape=jax.ShapeDtypeStruct((b, seq, width), BF16),
        scratch_shapes=scratch + [
            pltpu.VMEM((heads, tq, HEAD_DIM), F32),
            pltpu.VMEM((heads, tq, HEAD_DIM), F32),
        ],
        compiler_params=pltpu.CompilerParams(
            dimension_semantics=("arbitrary", "arbitrary"),
            vmem_limit_bytes=VMEM_LIMIT),
        name="sb_attention",
    )(*args)


def _out_proj_kernel(h_ref, yc_ref, ya_ref, w_ref, o_ref):
    half = yc_ref.shape[1]
    o_ref[...] = (h_ref[...]
                  + jnp.dot(yc_ref[...], w_ref[0:half, :], preferred_element_type=F32)
                  + jnp.dot(ya_ref[...], w_ref[half:, :], preferred_element_type=F32))


def _out_proj(h, yc, ya, w_out, *, tm):
    m, d = h.shape
    half = yc.shape[1]
    row = pl.BlockSpec((tm, d), lambda i: (i, 0))
    yrow = pl.BlockSpec((tm, half), lambda i: (i, 0))
    return pl.pallas_call(
        _out_proj_kernel,
        grid=(m // tm,),
        in_specs=[row, yrow, yrow, pl.BlockSpec(w_out.shape, lambda i: (0, 0))],
        out_specs=row,
        out_shape=jax.ShapeDtypeStruct((m, d), F32),
        compiler_params=pltpu.CompilerParams(
            dimension_semantics=("arbitrary",), vmem_limit_bytes=VMEM_LIMIT),
        name="out_proj",
    )(h, yc, ya, w_out)


def _mlp_kernel(h_ref, g_ref, w1_ref, w2_ref, o_ref, hn_ref, *, sub):
    f = pl.program_id(1)

    def ff_tile(hn):
        hid = jnp.dot(hn, w1_ref[...], preferred_element_type=F32)
        hid = jnp.square(jnp.maximum(hid, 0.0)).astype(BF16)
        return jnp.dot(hid, w2_ref[...], preferred_element_type=F32)

    @pl.when(f == 0)
    def _():
        for r in range(0, h_ref.shape[0], sub):
            rows = slice(r, r + sub)
            x = h_ref[rows, :]
            hn = _rms(x, g_ref[...]).astype(BF16)
            hn_ref[rows, :] = hn
            o_ref[rows, :] = x + ff_tile(hn)

    @pl.when(f > 0)
    def _():
        o_ref[...] += ff_tile(hn_ref[...])


def _mlp(h, g_mlp, w1, w2, *, tm, tf):
    m, d = h.shape
    d_ff = w1.shape[1]
    row = pl.BlockSpec((tm, d), lambda i, f: (i, 0))
    return pl.pallas_call(
        functools.partial(_mlp_kernel, sub=min(tm, MLP_FIRST_SUB_ROWS)),
        grid=(m // tm, d_ff // tf),
        in_specs=[row,
                  pl.BlockSpec((1, d), lambda i, f: (0, 0)),
                  pl.BlockSpec((d, tf), lambda i, f: (0, f)),
                  pl.BlockSpec((tf, d), lambda i, f: (f, 0))],
        out_specs=row,
        out_shape=jax.ShapeDtypeStruct((m, d), F32),
        scratch_shapes=[pltpu.VMEM((tm, d), BF16)],
        compiler_params=pltpu.CompilerParams(
            dimension_semantics=("arbitrary", "arbitrary"), vmem_limit_bytes=VMEM_LIMIT),
        name="mlp",
    )(h, g_mlp, w1, w2)


def _tiles(seq):
    return dict(tm_in=min(512, seq), tm_out=min(512, seq), tm_mlp=min(1024, seq), tf=1024,
                tq=min(256, seq), heads=4)


def kernel(x, meta_tokens, g_mix, w_in, conv_w, g_q, g_k, g_conv_out, g_attn_out, w_out,
           g_mlp, w_mlp_in, w_mlp_out):
    b, seq, d = x.shape
    depth = w_in.shape[0]
    n_meta = meta_tokens.shape[0]
    width = conv_w.shape[2]
    assert n_meta <= META_ROWS and seq % META_ROWS == 0
    cfg = _tiles(seq)
    meta_pad = META_ROWS - n_meta

    h = x.reshape(b * seq, d)
    hm = jnp.pad(meta_tokens.astype(x.dtype), ((meta_pad, 0), (0, 0)))
    zero_tail = jnp.zeros((SUBLANES, width), F32)
    later_w = (w_mlp_in, w_mlp_out, w_out)
    ride = all(_can_ride(a, b * seq // cfg["tm_in"]) for a in later_w + (w_in,))
    w_in_i = w_in[0].astype(BF16)

    for i in range(depth):
        g_mix_i = g_mix[i].reshape(1, d)
        g_co_i = g_conv_out[i].reshape(1, width)
        g_ao_i = g_attn_out[i].reshape(1, width)
        g_mlp_i = g_mlp[i].reshape(1, d)
        g_qk_i = jnp.stack([g_q[i] * (HEAD_DIM ** -0.5 * LOG2E), g_k[i]])
        last = i + 1 == depth
        riders = []
        if ride:
            riders = [(a, i) for a in later_w] + ([] if last else [(w_in, i + 1)])

        ycm, qm, km, vm, tail_m = _in_proj(hm, g_mix_i, w_in_i, conv_w[i], g_co_i, g_qk_i,
                                           zero_tail, tm=META_ROWS, tiles_per_seq=1)
        yc, q, k, v, _, *cast = _in_proj(h, g_mix_i, w_in_i, conv_w[i], g_co_i, g_qk_i,
                                         tail_m[0], riders,
                                         tm=cfg["tm_in"], tiles_per_seq=seq // cfg["tm_in"])
        if not ride:
            cast = [a[i].astype(BF16) for a in later_w] + ([] if last else [w_in[i + 1].astype(BF16)])
        w1_b, w2_b, w_out_b = cast[:3]
        ya = _attention(q.reshape(b, seq, width), k.reshape(b, seq, width),
                        v.reshape(b, seq, width), (km[None], vm[None]), g_ao_i,
                        tq=cfg["tq"], heads=cfg["heads"], own_pad=0, ctx_pad=meta_pad)
        h = _out_proj(h, yc, ya.reshape(b * seq, width), w_out_b, tm=cfg["tm_out"])
        h = _mlp(h, g_mlp_i, w1_b, w2_b, tm=cfg["tm_mlp"], tf=cfg["tf"])

        if not last:
            yam = _attention(qm[None], km[None], vm[None], None, g_ao_i,
                             tq=META_ROWS, heads=cfg["heads"], own_pad=meta_pad, ctx_pad=None)
            hm = _out_proj(hm, ycm, yam[0], w_out_b, tm=META_ROWS)
            hm = _mlp(hm, g_mlp_i, w1_b, w2_b, tm=META_ROWS, tf=cfg["tf"])
            w_in_i = cast[3]

    return h.reshape(b, seq, d)
```
